```python
import math
import jax, jax.numpy as jnp
from jax import lax
import numpy as np

D_MODEL = 1024
BATCH = 4
SEQ = 8192
DEPTH = 2

BLOCK = 128
N_EVEN = (DEPTH + 1) // 2
N_ODD = DEPTH // 2
DA_HEADS = 4
DA_HEAD_DIM = 64
DA_V_DIM = 2 * DA_HEAD_DIM
DA_WIDTH = DA_HEADS * DA_V_DIM
SB_HEADS = 8
SB_HEAD_DIM = 64
SB_WIDTH = SB_HEADS * SB_HEAD_DIM
MIX_WIDTH = DA_WIDTH + SB_WIDTH
IN_AB = 3 * DA_WIDTH + 3 * SB_WIDTH
SGU_CHUNK = 128
SGU_GROUPS = 8
SGU_WIDTH = D_MODEL
SGU_GROUP_CH = SGU_WIDTH // SGU_GROUPS
D_FF = 2816
EPS = 1e-6

kernel_name = "hybrid_diff_stickbreak_sgu_macaron"


def rms_norm(x, g):
    xf = x.astype(jnp.float32)
    y = xf * lax.rsqrt(jnp.mean(xf * xf, axis=-1, keepdims=True) + EPS)
    return (y * g.astype(jnp.float32)).astype(x.dtype)


def layer_norm(x, g, b):
    xf = x.astype(jnp.float32)
    mu = jnp.mean(xf, axis=-1, keepdims=True)
    xc = xf - mu
    y = xc * lax.rsqrt(jnp.mean(xc * xc, axis=-1, keepdims=True) + EPS)
    return (y * g.astype(jnp.float32) + b.astype(jnp.float32)).astype(x.dtype)


def swiglu(h, w_gate, w_up, w_down):
    return (jax.nn.silu(h @ w_gate) * (h @ w_up)) @ w_down


def alibi_slopes(n):
    return 2.0 ** (-8.0 * jnp.arange(1, n + 1, dtype=jnp.float32) / n)


def diff_stick_mixer(h, w_in, w_out, lam_p, g_subln, lambda_init):
    B, S, _ = h.shape
    nb = S // BLOCK
    proj = h @ w_in
    cuts = [DA_WIDTH, 2 * DA_WIDTH, 3 * DA_WIDTH, 3 * DA_WIDTH + SB_WIDTH, 3 * DA_WIDTH + 2 * SB_WIDTH]
    qa, ka, va, qb, kb, vb = jnp.split(proj, cuts, axis=-1)
    qa = qa.reshape(B, S, DA_HEADS, 2, DA_HEAD_DIM).transpose(0, 2, 3, 1, 4) * (DA_HEAD_DIM ** -0.5)
    ka = ka.reshape(B, S, DA_HEADS, 2, DA_HEAD_DIM).transpose(0, 2, 3, 1, 4)
    va = va.reshape(B, S, DA_HEADS, DA_V_DIM).transpose(0, 2, 1, 3)
    qb = qb.reshape(B, S, SB_HEADS, SB_HEAD_DIM).transpose(0, 2, 1, 3) * (SB_HEAD_DIM ** -0.5)
    kb = kb.reshape(B, S, SB_HEADS, SB_HEAD_DIM).transpose(0, 2, 1, 3)
    vb = vb.reshape(B, S, SB_HEADS, SB_HEAD_DIM).transpose(0, 2, 1, 3)

    lp = lam_p.astype(jnp.float32)
    lam = jnp.exp(jnp.sum(lp[0] * lp[1])) - jnp.exp(jnp.sum(lp[2] * lp[3])) + lambda_init
    slopes = alibi_slopes(DA_HEADS)
    key_pos = jnp.arange(S, dtype=jnp.int32)

    def block(i):
        start = i * BLOCK
        q_pos = start + jnp.arange(BLOCK, dtype=jnp.int32)
        rel = q_pos[:, None] - key_pos[None, :]
        qa_i = lax.dynamic_slice_in_dim(qa, start, BLOCK, axis=3)
        sa = jnp.einsum('bhmqd,bhmkd->bhmqk', qa_i, ka, preferred_element_type=jnp.float32)
        sa = sa - slopes[None, :, None, None, None] * rel.astype(jnp.float32)
        sa = jnp.where(rel >= 0, sa, -jnp.inf)
        p = jax.nn.softmax(sa, axis=-1)
        diff = p[:, :, 0] - lam * p[:, :, 1]
        oa = jnp.einsum('bhqk,bhkd->bhqd', diff.astype(va.dtype), va)
        qb_i = lax.dynamic_slice_in_dim(qb, start, BLOCK, axis=2)
        z = jnp.einsum('bhqd,bhkd->bhqk', qb_i, kb, preferred_element_type=jnp.float32)
        strict = rel > 0
        log_fail = jnp.where(strict, jax.nn.log_sigmoid(-z), 0.0)
        log_suffix = lax.cumsum(log_fail, axis=log_fail.ndim - 1, reverse=True) - log_fail
        wgt = jnp.where(strict, jnp.exp(jax.nn.log_sigmoid(z) + log_suffix), 0.0)
        ob = jnp.einsum('bhqk,bhkd->bhqd', wgt.astype(vb.dtype), vb)
        return oa, ob

    oa, ob = lax.map(block, jnp.arange(nb, dtype=jnp.int32))
    oa = oa.transpose(1, 0, 3, 2, 4).reshape(B, S, DA_HEADS, DA_V_DIM)
    ob = ob.transpose(1, 0, 3, 2, 4).reshape(B, S, SB_HEADS, SB_HEAD_DIM)
    oa = rms_norm(oa, g_subln) * (1.0 - lambda_init)
    o = jnp.concatenate([oa.reshape(B, S, DA_WIDTH), ob.reshape(B, S, SB_WIDTH)], axis=-1)
    return o @ w_out


def sgu_mixer(h, w_uv, b_uv, g_ln, b_ln, w_sp, b_sp, w_out):
    B, S, _ = h.shape
    zz = jax.nn.gelu(h @ w_uv + b_uv)
    u, v = jnp.split(zz, 2, axis=-1)
    v = layer_norm(v, g_ln, b_ln)
    v = v.reshape(B, S // SGU_CHUNK, SGU_CHUNK, SGU_GROUPS, SGU_GROUP_CH)
    causal = jnp.tril(jnp.ones((SGU_CHUNK, SGU_CHUNK), dtype=bool))
    w = jnp.where(causal[None], w_sp, 0.0).astype(v.dtype)
    mixed = jnp.einsum('gts,bnsgc->bntgc', w, v) + b_sp.T[None, None, :, :, None]
    out = u * mixed.reshape(B, S, SGU_WIDTH)
    return out @ w_out


def setup_inputs(seed: int = 0) -> dict:
    key = jax.random.key(seed)
    ks = jax.random.split(key, 16)
    n = jax.random.normal
    f32 = jnp.float32
    return {
        "x": n(ks[0], (BATCH, SEQ, D_MODEL), f32),
        "g_norm": 1.0 + 0.05 * n(ks[1], (DEPTH, 6, D_MODEL), f32),
        "w_ffn_gate": n(ks[2], (DEPTH, 2, D_MODEL, D_FF), f32) * D_MODEL ** -0.5,
        "w_ffn_up": n(ks[3], (DEPTH, 2, D_MODEL, D_FF), f32) * D_MODEL ** -0.5,
        "w_ffn_down": n(ks[4], (DEPTH, 2, D_FF, D_MODEL), f32) * D_FF ** -0.5,
        "w_in_ab": n(ks[5], (N_EVEN, D_MODEL, IN_AB), f32) * D_MODEL ** -0.5,
        "w_out_ab": n(ks[6], (N_EVEN, MIX_WIDTH, D_MODEL), f32) * MIX_WIDTH ** -0.5,
        "lambda_params": 0.1 * n(ks[7], (N_EVEN, 4, DA_HEAD_DIM), f32),
        "g_subln": 1.0 + 0.05 * n(ks[8], (N_EVEN, DA_V_DIM), f32),
        "w_uv": n(ks[9], (N_ODD, D_MODEL, 2 * SGU_WIDTH), f32) * D_MODEL ** -0.5,
        "b_uv": 0.02 * n(ks[10], (N_ODD, 2 * SGU_WIDTH), f32),
        "g_sgu_ln": 1.0 + 0.05 * n(ks[11], (N_ODD, SGU_WIDTH), f32),
        "b_sgu_ln": 0.02 * n(ks[12], (N_ODD, SGU_WIDTH), f32),
        "w_spatial": n(ks[13], (N_ODD, SGU_GROUPS, SGU_CHUNK, SGU_CHUNK), f32) * SGU_CHUNK ** -0.5,
        "b_spatial": 1.0 + 0.05 * n(ks[14], (N_ODD, SGU_GROUPS, SGU_CHUNK), f32),
        "w_out_c": n(ks[15], (N_ODD, SGU_WIDTH, D_MODEL), f32) * SGU_WIDTH ** -0.5,
    }


def reference(x, g_norm, w_ffn_gate, w_ffn_up, w_ffn_down, w_in_ab, w_out_ab,
              lambda_params, g_subln, w_uv, b_uv, g_sgu_ln, b_sgu_ln,
              w_spatial, b_spatial, w_out_c):
    for l in range(DEPTH):
        g = g_norm[l]
        f = swiglu(rms_norm(x, g[0]), w_ffn_gate[l, 0], w_ffn_up[l, 0], w_ffn_down[l, 0])
        x = x + 0.5 * rms_norm(f, g[1])
        h = rms_norm(x, g[2])
        if l % 2 == 0:
            e = l // 2
            lambda_init = 0.8 - 0.6 * math.exp(-0.3 * l)
            m = diff_stick_mixer(h, w_in_ab[e], w_out_ab[e], lambda_params[e], g_subln[e], lambda_init)
        else:
            o = l // 2
            m = sgu_mixer(h, w_uv[o], b_uv[o], g_sgu_ln[o], b_sgu_ln[o], w_spatial[o], b_spatial[o], w_out_c[o])
        x = x + rms_norm(m, g[3])
        f = swiglu(rms_norm(x, g[4]), w_ffn_gate[l, 1], w_ffn_up[l, 1], w_ffn_down[l, 1])
        x = x + 0.5 * rms_norm(f, g[5])
    return x
```

```python
import functools
import math

import jax
import jax.numpy as jnp
from jax import lax
from jax.experimental import pallas as pl
from jax.experimental.pallas import tpu as pltpu

F32 = jnp.float32
BF16 = jnp.bfloat16

EPS = 1e-6
LANES = 128
VMEM_LIMIT_BYTES = 52 * 1024 * 1024

DA_HEADS = 4
DA_HEAD_DIM = 64
SB_HEADS = 8
SB_HEAD_DIM = 64
SB_PAIRS = SB_HEADS * SB_HEAD_DIM // LANES
SGU_CHUNK = 128
SGU_GROUPS = 8


def _pick(n, prefs):
    for p in prefs:
        if n % p == 0:
            return p
    return n


def _rms(x, g):
    return x * lax.rsqrt(jnp.mean(x * x, axis=-1, keepdims=True) + EPS) * g


def _params(sem):
    return pltpu.CompilerParams(dimension_semantics=sem, vmem_limit_bytes=VMEM_LIMIT_BYTES)


def _ffn_kernel(x_ref, gpre_ref, gpost_ref, wg_ref, wu_ref, wd_ref, o_ref, h_ref, acc_ref):
    j = pl.program_id(1)

    @pl.when(j == 0)
    def _():
        h_ref[...] = _rms(x_ref[...], gpre_ref[...]).astype(BF16)
        acc_ref[...] = jnp.zeros_like(acc_ref)

    h = h_ref[...]
    gate = jnp.dot(h, wg_ref[...], preferred_element_type=F32)
    up = jnp.dot(h, wu_ref[...], preferred_element_type=F32)
    act = (gate * jax.nn.sigmoid(gate) * up).astype(BF16)
    acc_ref[...] += jnp.dot(act, wd_ref[...], preferred_element_type=F32)

    @pl.when(j == pl.num_programs(1) - 1)
    def _():
        o_ref[...] = x_ref[...] + 0.5 * _rms(acc_ref[...], gpost_ref[...])


def _ffn(x, g_pre, g_post, wg, wu, wd):
    n, d = x.shape
    f = wg.shape[1]
    tm = _pick(n, (512, 256, 128))
    tf = _pick(f, (1408, 512, 256, 128))
    return pl.pallas_call(
        _ffn_kernel,
        out_shape=jax.ShapeDtypeStruct((n, d), F32),
        grid=(n // tm, f // tf),
        in_specs=[
            pl.BlockSpec((tm, d), lambda i, j: (i, 0)),
            pl.BlockSpec((1, d), lambda i, j: (0, 0)),
            pl.BlockSpec((1, d), lambda i, j: (0, 0)),
            pl.BlockSpec((d, tf), lambda i, j: (0, j)),
            pl.BlockSpec((d, tf), lambda i, j: (0, j)),
            pl.BlockSpec((tf, d), lambda i, j: (j, 0)),
        ],
        out_specs=pl.BlockSpec((tm, d), lambda i, j: (i, 0)),
        scratch_shapes=[pltpu.VMEM((tm, d), BF16), pltpu.VMEM((tm, d), F32)],
        compiler_params=_params(("arbitrary", "arbitrary")),
        name="ffn",
    )(x, g_pre.reshape(1, d), g_post.reshape(1, d), wg, wu, wd)


def _inproj_kernel(x_ref, g_ref, w_ref, o_ref, h_ref):
    @pl.when(pl.program_id(1) == 0)
    def _():
        h_ref[...] = _rms(x_ref[...], g_ref[...]).astype(BF16)

    o_ref[...] = jnp.dot(h_ref[...], w_ref[...], preferred_element_type=F32).astype(o_ref.dtype)


def _inproj(x, g, w):
    n, d = x.shape
    c = w.shape[1]
    tm = _pick(n, (1024, 512, 256, 128))
    tn = _pick(c, (768, 512, 256, 128))
    return pl.pallas_call(
        _inproj_kernel,
        out_shape=jax.ShapeDtypeStruct((n, c), BF16),
        grid=(n // tm, c // tn),
        in_specs=[
            pl.BlockSpec((tm, d), lambda i, j: (i, 0)),
            pl.BlockSpec((1, d), lambda i, j: (0, 0)),
            pl.BlockSpec((d, tn), lambda i, j: (0, j)),
        ],
        out_specs=pl.BlockSpec((tm, tn), lambda i, j: (i, j)),
        scratch_shapes=[pltpu.VMEM((tm, d), BF16)],
        compiler_params=_params(("arbitrary", "arbitrary")),
        name="inproj",
    )(x, g.reshape(1, d), w)


def _split_pair(q):
    qf = q.astype(F32) * (DA_HEAD_DIM ** -0.5)
    lane = lax.broadcasted_iota(jnp.int32, qf.shape, 1)
    lo = jnp.where(lane < DA_HEAD_DIM, qf, 0.0).astype(BF16)
    hi = jnp.where(lane >= DA_HEAD_DIM, qf, 0.0).astype(BF16)
    return lo, hi


def _qk(q, k):
    return lax.dot_general(q, k, (((1,), (1,)), ((), ())), preferred_element_type=F32)


def _diff_kernel(q_ref, k_ref, v_ref, lp_ref, g_ref, o_ref, m_ref, l_ref, a_ref, *, tq, tk, lambda_init):
    h = pl.program_id(1)
    qi = pl.program_id(2)
    ratio = tq // tk
    qs = _split_pair(q_ref[0])

    lane_i = lax.broadcasted_iota(jnp.int32, (1, tk), 1)
    head_v = (jnp.zeros((1, tk), jnp.int32) + (h + 1)).astype(F32)
    slope = jnp.exp2(-8.0 / DA_HEADS * head_v)
    row_i = lax.broadcasted_iota(jnp.int32, (tq, tk), 0)
    col_i = lax.broadcasted_iota(jnp.int32, (tq, tk), 1)

    m_ref[...] = jnp.full(m_ref.shape, -jnp.inf, F32)
    l_ref[...] = jnp.zeros_like(l_ref)
    a_ref[...] = jnp.zeros_like(a_ref)

    def tile(kj, mask):
        s0 = pl.multiple_of(kj * tk, tk)
        k = k_ref[0, pl.ds(s0, tk), :]
        v = v_ref[0, pl.ds(s0, tk), :]
        bias = slope * (lane_i + (s0 - qi * tq)).astype(F32)
        for mp in range(2):
            s = _qk(qs[mp], k) + bias
            if mask is not None:
                s = jnp.where(mask, s, -jnp.inf)
            m_old = m_ref[mp]
            m_new = jnp.maximum(m_old, jnp.max(s, axis=-1, keepdims=True))
            p = jnp.exp(s - m_new)
            alpha = jnp.exp(m_old - m_new)
            l_ref[mp] = alpha * l_ref[mp] + jnp.sum(p, axis=-1, keepdims=True)
            a_ref[mp] = alpha * a_ref[mp] + jnp.dot(p.astype(BF16), v, preferred_element_type=F32)
            m_ref[mp] = m_new

    for i in range(ratio):
        tile(qi * ratio + i, col_i + i * tk <= row_i)

    def body(kj, c):
        tile(kj, None)
        return c

    lax.fori_loop(0, qi * ratio, body, 0)

    lp = lp_ref[...]
    lam = (jnp.exp(jnp.sum(lp[0:1] * lp[1:2], axis=-1, keepdims=True))
           - jnp.exp(jnp.sum(lp[2:3] * lp[3:4], axis=-1, keepdims=True)) + lambda_init)
    o1 = a_ref[0] / l_ref[0]
    o2 = a_ref[1] / l_ref[1]
    oa = o1 - lam * o2
    o_ref[0] = (_rms(oa, g_ref[...]) * (1.0 - lambda_init)).astype(o_ref.dtype)


def _diff_attn(proj, lam_p, g_subln, lambda_init):
    b, s, _ = proj.shape
    tq = _pick(s, (256, 128))
    tk = tq
    kern = functools.partial(_diff_kernel, tq=tq, tk=tk, lambda_init=lambda_init)
    return pl.pallas_call(
        kern,
        out_shape=jax.ShapeDtypeStruct((b, s, DA_HEADS * LANES), BF16),
        grid=(b, DA_HEADS, s // tq),
        in_specs=[
            pl.BlockSpec((1, tq, LANES), lambda bi, h, qi: (bi, qi, h)),
            pl.BlockSpec((1, s, LANES), lambda bi, h, qi: (bi, 0, DA_HEADS + h)),
            pl.BlockSpec((1, s, LANES), lambda bi, h, qi: (bi, 0, 2 * DA_HEADS + h)),
            pl.BlockSpec(lam_p.shape, lambda bi, h, qi: (0, 0)),
            pl.BlockSpec((1, LANES), lambda bi, h, qi: (0, 0)),
        ],
        out_specs=pl.BlockSpec((1, tq, LANES), lambda bi, h, qi: (bi, qi, h)),
        scratch_shapes=[
            pltpu.VMEM((2, tq, 1), F32),
            pltpu.VMEM((2, tq, 1), F32),
            pltpu.VMEM((2, tq, LANES), F32),
        ],
        compiler_params=_params(("arbitrary", "arbitrary", "arbitrary")),
        name="diff_attn",
    )(proj, proj, proj, lam_p, g_subln.reshape(1, LANES))


def _sb_kernel(q_ref, k_ref, v_ref, w_ref, o_ref, c_ref, a_ref, *, tq, tk):
    qi = pl.program_id(2)
    ratio = tq // tk
    nsub = tk // LANES
    qs = _split_pair(q_ref[0])
    row_i = lax.broadcasted_iota(jnp.int32, (tq, LANES), 0)
    col_i = lax.broadcasted_iota(jnp.int32, (tq, LANES), 1)

    c_ref[...] = jnp.zeros_like(c_ref)
    a_ref[...] = jnp.zeros_like(a_ref)

    def tile(kj, diag):
        s0 = pl.multiple_of(kj * tk, tk)
        k = k_ref[0, pl.ds(s0, tk), :]
        v = v_ref[0, pl.ds(s0, tk), :]
        wmat = w_ref[...]
        for hh in range(2):
            z = _qk(qs[hh], k)
            carry = c_ref[hh]
            wts = [None] * nsub
            for sb in reversed(range(nsub)):
                zb = z[:, sb * LANES:(sb + 1) * LANES]
                sp = jnp.maximum(zb, 0.0) + jnp.log(1.0 + jnp.exp(-jnp.abs(zb)))
                if diag is not None:
                    strict = col_i + (diag * tk + sb * LANES) < row_i
                    sp = jnp.where(strict, sp, 0.0)
                hi = sp.astype(BF16)
                lo = (sp - hi.astype(F32)).astype(BF16)
                ce = jnp.dot(jnp.concatenate([hi, lo], axis=1), wmat, preferred_element_type=F32)
                w = jnp.exp(zb - sp + ce[:, :LANES] + carry)
                if diag is not None:
                    w = jnp.where(strict, w, 0.0)
                wts[sb] = w.astype(BF16)
                carry = carry + ce[:, LANES:]
            c_ref[hh] = carry
            wt = wts[0] if nsub == 1 else jnp.concatenate(wts, axis=1)
            a_ref[hh] += jnp.dot(wt, v, preferred_element_type=F32)

    for i in reversed(range(ratio)):
        tile(qi * ratio + i, i)

    def body(it, c):
        tile(qi * ratio - 1 - it, None)
        return c

    lax.fori_loop(0, qi * ratio, body, 0)

    o_ref[0] = jnp.where(col_i < SB_HEAD_DIM, a_ref[0], a_ref[1]).astype(o_ref.dtype)


def _suffix_matrix():
    j = jnp.arange(2 * LANES)[:, None] % LANES
    s = jnp.arange(2 * LANES)[None, :]
    return -jnp.where(s < LANES, (j > s).astype(F32), 1.0).astype(BF16)


def _sb_attn(proj):
    b, s, _ = proj.shape
    tq = _pick(s, (256, 128))
    tk = tq
    base = 3 * DA_HEADS
    kern = functools.partial(_sb_kernel, tq=tq, tk=tk)
    return pl.pallas_call(
        kern,
        out_shape=jax.ShapeDtypeStruct((b, s, SB_PAIRS * LANES), BF16),
        grid=(b, SB_PAIRS, s // tq),
        in_specs=[
            pl.BlockSpec((1, tq, LANES), lambda bi, p, qi: (bi, qi, base + p)),
            pl.BlockSpec((1, s, LANES), lambda bi, p, qi: (bi, 0, base + SB_PAIRS + p)),
            pl.BlockSpec((1, s, LANES), lambda bi, p, qi: (bi, 0, base + 2 * SB_PAIRS + p)),
            pl.BlockSpec((2 * LANES, 2 * LANES), lambda bi, p, qi: (0, 0)),
        ],
        out_specs=pl.BlockSpec((1, tq, LANES), lambda bi, p, qi: (bi, qi, p)),
        scratch_shapes=[
            pltpu.VMEM((2, tq, LANES), F32),
            pltpu.VMEM((2, tq, LANES), F32),
        ],
        compiler_params=_params(("arbitrary", "arbitrary", "arbitrary")),
        name="sb_attn",
    )(proj, proj, proj, _suffix_matrix())


def _outproj_kernel(x_ref, oa_ref, ob_ref, wa_ref, wb_ref, g_ref, o_ref):
    m = (jnp.dot(oa_ref[...], wa_ref[...], preferred_element_type=F32)
         + jnp.dot(ob_ref[...], wb_ref[...], preferred_element_type=F32))
    o_ref[...] = x_ref[...] + _rms(m, g_ref[...])


def _outproj(x, oa, ob, w, g):
    n, d = x.shape
    ca, cb = oa.shape[1], ob.shape[1]
    tm = _pick(n, (1024, 512, 256, 128))
    return pl.pallas_call(
        _outproj_kernel,
        out_shape=jax.ShapeDtypeStruct((n, d), F32),
        grid=(n // tm,),
        in_specs=[
            pl.BlockSpec((tm, d), lambda i: (i, 0)),
            pl.BlockSpec((tm, ca), lambda i: (i, 0)),
            pl.BlockSpec((tm, cb), lambda i: (i, 0)),
            pl.BlockSpec((ca, d), lambda i: (0, 0)),
            pl.BlockSpec((cb, d), lambda i: (0, 0)),
            pl.BlockSpec((1, d), lambda i: (0, 0)),
        ],
        out_specs=pl.BlockSpec((tm, d), lambda i: (i, 0)),
        compiler_params=_params(("arbitrary",)),
        name="outproj",
    )(x, oa, ob, w[:ca], w[ca:], g.reshape(1, d))


def _sgu_kernel(x_ref, gpre_ref, gpost_ref, wuv_ref, buv_ref, gln_ref, bln_ref, wsp_ref, bsp_ref, wo_ref, o_ref,
                *, tm, width):
    x = x_ref[...]
    h = _rms(x, gpre_ref[...]).astype(BF16)
    zz = jax.nn.gelu(jnp.dot(h, wuv_ref[...], preferred_element_type=F32) + buv_ref[...])
    u = zz[:, :width]
    v = zz[:, width:]
    mu = jnp.mean(v, axis=-1, keepdims=True)
    vc = v - mu
    v = (vc * lax.rsqrt(jnp.mean(vc * vc, axis=-1, keepdims=True) + EPS) * gln_ref[...] + bln_ref[...]).astype(BF16)

    t_i = lax.broadcasted_iota(jnp.int32, (SGU_CHUNK, SGU_CHUNK), 0)
    s_i = lax.broadcasted_iota(jnp.int32, (SGU_CHUNK, SGU_CHUNK), 1)
    gch = width // SGU_GROUPS
    cols = []
    for g in range(SGU_GROUPS):
        wg = jnp.where(s_i <= t_i, wsp_ref[g], 0.0).astype(BF16)
        rows = [jnp.dot(wg, v[n * SGU_CHUNK:(n + 1) * SGU_CHUNK, g * gch:(g + 1) * gch],
                        preferred_element_type=F32) for n in range(tm // SGU_CHUNK)]
        cols.append(jnp.concatenate(rows, axis=0))
    mixed = jnp.concatenate(cols, axis=1) + bsp_ref[...]
    y = jnp.dot((u * mixed).astype(BF16), wo_ref[...], preferred_element_type=F32)
    o_ref[...] = x + _rms(y, gpost_ref[...])


def _sgu(x, g_pre, g_post, w_uv, b_uv, g_ln, b_ln, w_sp, b_sp, w_out):
    n, d = x.shape
    width = w_out.shape[0]
    tm = _pick(n, (512, 256, 128))
    gch = width // SGU_GROUPS
    bias = jnp.tile(jnp.repeat(b_sp.T, gch, axis=1), (tm // SGU_CHUNK, 1))
    kern = functools.partial(_sgu_kernel, tm=tm, width=width)
    const = lambda i: (0, 0)
    return pl.pallas_call(
        kern,
        out_shape=jax.ShapeDtypeStruct((n, d), F32),
        grid=(n // tm,),
        in_specs=[
            pl.BlockSpec((tm, d), lambda i: (i, 0)),
            pl.BlockSpec((1, d), const),
            pl.BlockSpec((1, d), const),
            pl.BlockSpec((d, 2 * width), const),
            pl.BlockSpec((1, 2 * width), const),
            pl.BlockSpec((1, width), const),
            pl.BlockSpec((1, width), const),
            pl.BlockSpec(w_sp.shape, lambda i: (0, 0, 0)),
            pl.BlockSpec((tm, width), const),
            pl.BlockSpec((width, d), const),
        ],
        out_specs=pl.BlockSpec((tm, d), lambda i: (i, 0)),
        compiler_params=_params(("arbitrary",)),
        name="sgu",
    )(x, g_pre.reshape(1, d), g_post.reshape(1, d), w_uv, b_uv.reshape(1, -1), g_ln.reshape(1, -1),
      b_ln.reshape(1, -1), w_sp, bias, w_out)


def kernel(x, g_norm, w_ffn_gate, w_ffn_up, w_ffn_down, w_in_ab, w_out_ab, lambda_params, g_subln,
           w_uv, b_uv, g_sgu_ln, b_sgu_ln, w_spatial, b_spatial, w_out_c):
    b, s, d = x.shape
    depth = g_norm.shape[0]
    assert w_in_ab.shape[2] == 3 * LANES * (DA_HEADS + SB_PAIRS)
    assert g_subln.shape[1] == LANES and w_spatial.shape[1:] == (SGU_GROUPS, SGU_CHUNK, SGU_CHUNK)
    cast = lambda w: w.astype(BF16)
    xf = x.reshape(b * s, d)
    for l in range(depth):
        g = g_norm[l]
        xf = _ffn(xf, g[0], g[1], cast(w_ffn_gate[l, 0]), cast(w_ffn_up[l, 0]), cast(w_ffn_down[l, 0]))
        if l % 2 == 0:
            e = l // 2
            lambda_init = 0.8 - 0.6 * math.exp(-0.3 * l)
            proj = _inproj(xf, g[2], cast(w_in_ab[e])).reshape(b, s, -1)
            oa = _diff_attn(proj, lambda_params[e], g_subln[e], lambda_init)
            ob = _sb_attn(proj)
            xf = _outproj(xf, oa.reshape(b * s, -1), ob.reshape(b * s, -1), cast(w_out_ab[e]), g[3])
        else:
            o = l // 2
            xf = _sgu(xf, g[2], g[3], cast(w_uv[o]), b_uv[o], g_sgu_ln[o], b_sgu_ln[o], w_spatial[o],
                      b_spatial[o], cast(w_out_c[o]))
        xf = _ffn(xf, g[4], g[5], cast(w_ffn_gate[l, 1]), cast(w_ffn_up[l, 1]), cast(w_ffn_down[l, 1]))
    return xf.reshape(b, s, d)
```

```python
import functools
import math

import jax
import jax.numpy as jnp
from jax import lax
from jax.experimental import pallas as pl
from jax.experimental.pallas import tpu as pltpu

F32 = jnp.float32
BF16 = jnp.bfloat16

EPS = 1e-6
LANES = 128
VMEM_LIMIT_BYTES = 52 * 1024 * 1024

DA_HEADS = 4
DA_HEAD_DIM = 64
SB_HEADS = 8
SB_HEAD_DIM = 64
SB_PAIRS = SB_HEADS * SB_HEAD_DIM // LANES
SGU_CHUNK = 128
SGU_GROUPS = 8
ATTN_TILE = 512
SB_KEY_TILE = 256
POS_RADIX = 64


def _pick(n, prefs):
    for p in prefs:
        if n % p == 0:
            return p
    return n


def _rms(x, g):
    return x * lax.rsqrt(jnp.mean(x * x, axis=-1, keepdims=True) + EPS) * g


def _params(sem):
    return pltpu.CompilerParams(dimension_semantics=sem, vmem_limit_bytes=VMEM_LIMIT_BYTES)


def _ffn_kernel(x_ref, gpre_ref, gpost_ref, wg_ref, wu_ref, wd_ref, o_ref, h_ref, acc_ref):
    j = pl.program_id(1)

    @pl.when(j == 0)
    def _():
        h_ref[...] = _rms(x_ref[...], gpre_ref[...]).astype(BF16)
        acc_ref[...] = jnp.zeros_like(acc_ref)

    h = h_ref[...]
    gate = jnp.dot(h, wg_ref[...], preferred_element_type=F32)
    up = jnp.dot(h, wu_ref[...], preferred_element_type=F32)
    act = (gate * jax.nn.sigmoid(gate) * up).astype(BF16)
    acc_ref[...] += jnp.dot(act, wd_ref[...], preferred_element_type=F32)

    @pl.when(j == pl.num_programs(1) - 1)
    def _():
        o_ref[...] = x_ref[...] + 0.5 * _rms(acc_ref[...], gpost_ref[...])


def _ffn(x, g_pre, g_post, wg, wu, wd):
    n, d = x.shape
    f = wg.shape[1]
    tm = _pick(n, (512, 256, 128))
    tf = _pick(f, (1408, 512, 256, 128))
    return pl.pallas_call(
        _ffn_kernel,
        out_shape=jax.ShapeDtypeStruct((n, d), F32),
        grid=(n // tm, f // tf),
        in_specs=[
            pl.BlockSpec((tm, d), lambda i, j: (i, 0)),
            pl.BlockSpec((1, d), lambda i, j: (0, 0)),
            pl.BlockSpec((1, d), lambda i, j: (0, 0)),
            pl.BlockSpec((d, tf), lambda i, j: (0, j)),
            pl.BlockSpec((d, tf), lambda i, j: (0, j)),
            pl.BlockSpec((tf, d), lambda i, j: (j, 0)),
        ],
        out_specs=pl.BlockSpec((tm, d), lambda i, j: (i, 0)),
        scratch_shapes=[pltpu.VMEM((tm, d), BF16), pltpu.VMEM((tm, d), F32)],
        compiler_params=_params(("arbitrary", "arbitrary")),
        name="ffn",
    )(x, g_pre.reshape(1, d), g_post.reshape(1, d), wg, wu, wd)


def _inproj_kernel(x_ref, g_ref, w_ref, o_ref, h_ref):
    @pl.when(pl.program_id(1) == 0)
    def _():
        h_ref[...] = _rms(x_ref[...], g_ref[...]).astype(BF16)

    o_ref[...] = jnp.dot(h_ref[...], w_ref[...], preferred_element_type=F32).astype(o_ref.dtype)


def _inproj(x, g, w):
    n, d = x.shape
    c = w.shape[1]
    tm = _pick(n, (1024, 512, 256, 128))
    tn = _pick(c, (768, 512, 256, 128))
    return pl.pallas_call(
        _inproj_kernel,
        out_shape=jax.ShapeDtypeStruct((n, c), BF16),
        grid=(n // tm, c // tn),
        in_specs=[
            pl.BlockSpec((tm, d), lambda i, j: (i, 0)),
            pl.BlockSpec((1, d), lambda i, j: (0, 0)),
            pl.BlockSpec((d, tn), lambda i, j: (0, j)),
        ],
        out_specs=pl.BlockSpec((tm, tn), lambda i, j: (i, j)),
        scratch_shapes=[pltpu.VMEM((tm, d), BF16)],
        compiler_params=_params(("arbitrary", "arbitrary")),
        name="inproj",
    )(x, g.reshape(1, d), w)


def _split_pair(q):
    qf = q.astype(F32) * (DA_HEAD_DIM ** -0.5)
    lane = lax.broadcasted_iota(jnp.int32, qf.shape, 1)
    lo = jnp.where(lane < DA_HEAD_DIM, qf, 0.0).astype(BF16)
    hi = jnp.where(lane >= DA_HEAD_DIM, qf, 0.0).astype(BF16)
    return lo, hi


def _qk(q, k):
    return lax.dot_general(q, k, (((1,), (1,)), ((), ())), preferred_element_type=F32)


def _sweep_pairs(n, heavy, combine):
    def body(i, c):
        first = heavy(2 * i)
        second = heavy(2 * i + 1)
        combine(first)
        combine(second)
        return c

    lax.fori_loop(0, n // 2, body, 0)

    @pl.when(n % 2 == 1)
    def _():
        combine(heavy(n - 1))


def _diff_kernel(q_ref, k_ref, v_ref, pos_ref, lp_ref, g_ref, o_ref, m_ref, l_ref, a_ref, *, t, lambda_init):
    h = pl.program_id(1)
    qi = pl.program_id(2)
    lane = lax.broadcasted_iota(jnp.int32, (t, LANES), 1)
    head_v = (jnp.zeros((t, LANES), jnp.int32) + (h + 1)).astype(F32)
    slope = jnp.exp2(-8.0 / DA_HEADS * head_v)
    t0 = (jnp.zeros((t, LANES), jnp.int32) + qi * t).astype(F32)
    aug = jnp.where(lane == 0, slope * POS_RADIX,
                    jnp.where(lane == 1, slope, jnp.where(lane == 2, -slope * t0, 0.0))).astype(BF16)
    q_aug = [jnp.concatenate([qm, aug], axis=1) for qm in _split_pair(q_ref[0])]
    ones = jnp.ones((t, LANES), BF16)
    row_i = lax.broadcasted_iota(jnp.int32, (t, t), 0)
    col_i = lax.broadcasted_iota(jnp.int32, (t, t), 1)

    def trip(kjs, mask=None):
        starts = [pl.multiple_of(kj * t, t) for kj in kjs]
        ks = [jnp.concatenate([k_ref[0, pl.ds(s0, t), :], pos_ref[pl.ds(s0, t), :]], axis=1) for s0 in starts]
        vs = [jnp.concatenate([v_ref[0, pl.ds(s0, t), :], ones], axis=1) for s0 in starts]
        for mp in range(2):
            ss = [_qk(q_aug[mp], k) for k in ks]
            if mask is not None:
                ss = [jnp.where(mask, s, -jnp.inf) for s in ss]
            mloc = functools.reduce(jnp.maximum, [jnp.max(s, axis=-1, keepdims=True) for s in ss])
            m_new = jnp.broadcast_to(mloc, (t, LANES))
            if mask is None:
                m_old = m_ref[mp]
                m_new = jnp.maximum(m_old, m_new)
            m_wide = jnp.concatenate([m_new] * (t // LANES), axis=1)
            pv = sum(jnp.dot(jnp.exp(s - m_wide).astype(BF16), v, preferred_element_type=F32)
                     for s, v in zip(ss, vs))
            if mask is None:
                alpha = jnp.exp(m_old - m_new)
                a_ref[mp] = alpha * a_ref[mp] + pv[:, :LANES]
                l_ref[mp] = alpha * l_ref[mp] + pv[:, LANES:]
            else:
                a_ref[mp] = pv[:, :LANES]
                l_ref[mp] = pv[:, LANES:]
            m_ref[mp] = m_new

    trip([qi], col_i <= row_i)

    def body(i, c):
        trip([2 * i, 2 * i + 1])
        return c

    lax.fori_loop(0, qi // 2, body, 0)

    @pl.when(qi % 2 == 1)
    def _():
        trip([qi - 1])

    lp = lp_ref[...]
    lam = (jnp.exp(jnp.sum(lp[0:1] * lp[1:2], axis=-1, keepdims=True))
           - jnp.exp(jnp.sum(lp[2:3] * lp[3:4], axis=-1, keepdims=True)) + lambda_init)
    oa = a_ref[0] / l_ref[0] - lam * (a_ref[1] / l_ref[1])
    o_ref[0] = (_rms(oa, g_ref[...]) * (1.0 - lambda_init)).astype(o_ref.dtype)


def _key_positions(s):
    pos = jnp.arange(s, dtype=jnp.int32)[:, None]
    lane = jnp.arange(LANES, dtype=jnp.int32)[None, :]
    tab = jnp.where(lane == 0, pos // POS_RADIX, jnp.where(lane == 1, pos % POS_RADIX, (lane == 2).astype(jnp.int32)))
    return tab.astype(BF16)


def _diff_attn(proj, lam_p, g_subln, lambda_init):
    b, s, _ = proj.shape
    t = _pick(s, (ATTN_TILE, LANES))
    assert s // POS_RADIX <= 256 and s // t <= 256
    kern = functools.partial(_diff_kernel, t=t, lambda_init=lambda_init)
    return pl.pallas_call(
        kern,
        out_shape=jax.ShapeDtypeStruct((b, s, DA_HEADS * LANES), BF16),
        grid=(b, DA_HEADS, s // t),
        in_specs=[
            pl.BlockSpec((1, t, LANES), lambda bi, h, qi: (bi, qi, h)),
            pl.BlockSpec((1, s, LANES), lambda bi, h, qi: (bi, 0, DA_HEADS + h)),
            pl.BlockSpec((1, s, LANES), lambda bi, h, qi: (bi, 0, 2 * DA_HEADS + h)),
            pl.BlockSpec((s, LANES), lambda bi, h, qi: (0, 0)),
            pl.BlockSpec(lam_p.shape, lambda bi, h, qi: (0, 0)),
            pl.BlockSpec((1, LANES), lambda bi, h, qi: (0, 0)),
        ],
        out_specs=pl.BlockSpec((1, t, LANES), lambda bi, h, qi: (bi, qi, h)),
        scratch_shapes=[pltpu.VMEM((2, t, LANES), F32)] * 3,
        compiler_params=_params(("arbitrary", "arbitrary", "arbitrary")),
        name="diff_attn",
    )(proj, proj, proj, _key_positions(s), lam_p, g_subln.reshape(1, LANES))


def _sb_kernel(q_ref, k_ref, v_ref, w_ref, o_ref, c_ref, a_ref, *, tq, tk):
    qi = pl.program_id(2)
    ratio = tq // tk
    nsub = tk // LANES
    qs = _split_pair(q_ref[0])
    row_i = lax.broadcasted_iota(jnp.int32, (tq, LANES), 0)
    col_i = lax.broadcasted_iota(jnp.int32, (tq, LANES), 1)

    def heavy(kj, diag=None):
        s0 = pl.multiple_of(kj * tk, tk)
        k = k_ref[0, pl.ds(s0, tk), :]
        v = v_ref[0, pl.ds(s0, tk), :]
        wmat = w_ref[...]
        outs = []
        for hh in range(2):
            z = _qk(qs[hh], k)
            later = None
            wts = [None] * nsub
            for sb in reversed(range(nsub)):
                zb = z[:, sb * LANES:(sb + 1) * LANES]
                sp = jnp.maximum(zb, 0.0) + jnp.log(1.0 + jnp.exp(-jnp.abs(zb)))
                if diag is not None:
                    strict = col_i + (diag + sb * LANES) < row_i
                    sp = jnp.where(strict, sp, 0.0)
                hi = sp.astype(BF16)
                lo = (sp - hi.astype(F32)).astype(BF16)
                ce = jnp.dot(jnp.concatenate([hi, lo], axis=1), wmat, preferred_element_type=F32)
                arg = zb + ce[:, :LANES]
                if later is not None:
                    arg = arg + later
                w = jnp.exp(arg)
                if diag is not None:
                    w = jnp.where(strict, w, 0.0)
                wts[sb] = w.astype(BF16)
                later = ce[:, LANES:] if later is None else later + ce[:, LANES:]
            wt = wts[0] if nsub == 1 else jnp.concatenate(wts, axis=1)
            outs.append((jnp.dot(wt, v, preferred_element_type=F32), later))
        return outs

    def combine(outs):
        for hh, (pv, total) in enumerate(outs):
            carry = c_ref[hh]
            a_ref[hh] += jnp.exp(carry) * pv
            c_ref[hh] = carry + total

    for hh, (pv, total) in enumerate(heavy(qi * ratio + ratio - 1, (ratio - 1) * tk)):
        a_ref[hh] = pv
        c_ref[hh] = total
    for i in reversed(range(ratio - 1)):
        combine(heavy(qi * ratio + i, i * tk))
    _sweep_pairs(qi * ratio, lambda i: heavy(qi * ratio - 1 - i), combine)

    o_ref[0] = jnp.where(col_i < SB_HEAD_DIM, a_ref[0], a_ref[1]).astype(o_ref.dtype)


def _suffix_matrix():
    j = jnp.arange(2 * LANES)[:, None] % LANES
    s = jnp.arange(2 * LANES)[None, :]
    return -jnp.where(s < LANES, (j >= s).astype(F32), 1.0).astype(BF16)


def _sb_attn(proj):
    b, s, _ = proj.shape
    t = _pick(s, (ATTN_TILE, SB_KEY_TILE, LANES))
    tk = min(t, SB_KEY_TILE)
    base = 3 * DA_HEADS
    kern = functools.partial(_sb_kernel, tq=t, tk=tk)
    return pl.pallas_call(
        kern,
        out_shape=jax.ShapeDtypeStruct((b, s, SB_PAIRS * LANES), BF16),
        grid=(b, SB_PAIRS, s // t),
        in_specs=[
            pl.BlockSpec((1, t, LANES), lambda bi, p, qi: (bi, qi, base + p)),
            pl.BlockSpec((1, s, LANES), lambda bi, p, qi: (bi, 0, base + SB_PAIRS + p)),
            pl.BlockSpec((1, s, LANES), lambda bi, p, qi: (bi, 0, base + 2 * SB_PAIRS + p)),
            pl.BlockSpec((2 * LANES, 2 * LANES), lambda bi, p, qi: (0, 0)),
        ],
        out_specs=pl.BlockSpec((1, t, LANES), lambda bi, p, qi: (bi, qi, p)),
        scratch_shapes=[pltpu.VMEM((2, t, LANES), F32)] * 2,
        compiler_params=_params(("arbitrary", "arbitrary", "arbitrary")),
        name="sb_attn",
    )(proj, proj, proj, _suffix_matrix())


def _outproj_kernel(x_ref, oa_ref, ob_ref, wa_ref, wb_ref, g_ref, o_ref):
    m = (jnp.dot(oa_ref[...], wa_ref[...], preferred_element_type=F32)
         + jnp.dot(ob_ref[...], wb_ref[...], preferred_element_type=F32))
    o_ref[...] = x_ref[...] + _rms(m, g_ref[...])


def _outproj(x, oa, ob, w, g):
    n, d = x.shape
    ca, cb = oa.shape[1], ob.shape[1]
    tm = _pick(n, (1024, 512, 256, 128))
    return pl.pallas_call(
        _outproj_kernel,
        out_shape=jax.ShapeDtypeStruct((n, d), F32),
        grid=(n // tm,),
        in_specs=[
            pl.BlockSpec((tm, d), lambda i: (i, 0)),
            pl.BlockSpec((tm, ca), lambda i: (i, 0)),
            pl.BlockSpec((tm, cb), lambda i: (i, 0)),
            pl.BlockSpec((ca, d), lambda i: (0, 0)),
            pl.BlockSpec((cb, d), lambda i: (0, 0)),
            pl.BlockSpec((1, d), lambda i: (0, 0)),
        ],
        out_specs=pl.BlockSpec((tm, d), lambda i: (i, 0)),
        compiler_params=_params(("arbitrary",)),
        name="outproj",
    )(x, oa, ob, w[:ca], w[ca:], g.reshape(1, d))


def _sgu_kernel(x_ref, gpre_ref, gpost_ref, wuv_ref, buv_ref, gln_ref, bln_ref, wsp_ref, bsp_ref, wo_ref, o_ref,
                *, tm, width):
    x = x_ref[...]
    h = _rms(x, gpre_ref[...]).astype(BF16)
    zz = jax.nn.gelu(jnp.dot(h, wuv_ref[...], preferred_element_type=F32) + buv_ref[...])
    u = zz[:, :width]
    v = zz[:, width:]
    mu = jnp.mean(v, axis=-1, keepdims=True)
    vc = v - mu
    v = (vc * lax.rsqrt(jnp.mean(vc * vc, axis=-1, keepdims=True) + EPS) * gln_ref[...] + bln_ref[...]).astype(BF16)

    t_i = lax.broadcasted_iota(jnp.int32, (SGU_CHUNK, SGU_CHUNK), 0)
    s_i = lax.broadcasted_iota(jnp.int32, (SGU_CHUNK, SGU_CHUNK), 1)
    gch = width // SGU_GROUPS
    cols = []
    for g in range(SGU_GROUPS):
        wg = jnp.where(s_i <= t_i, wsp_ref[g], 0.0).astype(BF16)
        rows = [jnp.dot(wg, v[n * SGU_CHUNK:(n + 1) * SGU_CHUNK, g * gch:(g + 1) * gch],
                        preferred_element_type=F32) for n in range(tm // SGU_CHUNK)]
        cols.append(jnp.concatenate(rows, axis=0))
    mixed = jnp.concatenate(cols, axis=1) + bsp_ref[...]
    y = jnp.dot((u * mixed).astype(BF16), wo_ref[...], preferred_element_type=F32)
    o_ref[...] = x + _rms(y, gpost_ref[...])


def _sgu(x, g_pre, g_post, w_uv, b_uv, g_ln, b_ln, w_sp, b_sp, w_out):
    n, d = x.shape
    width = w_out.shape[0]
    tm = _pick(n, (512, 256, 128))
    gch = width // SGU_GROUPS
    bias = jnp.tile(jnp.repeat(b_sp.T, gch, axis=1), (tm // SGU_CHUNK, 1))
    kern = functools.partial(_sgu_kernel, tm=tm, width=width)
    const = lambda i: (0, 0)
    return pl.pallas_call(
        kern,
        out_shape=jax.ShapeDtypeStruct((n, d), F32),
        grid=(n // tm,),
        in_specs=[
            pl.BlockSpec((tm, d), lambda i: (i, 0)),
            pl.BlockSpec((1, d), const),
            pl.BlockSpec((1, d), const),
            pl.BlockSpec((d, 2 * width), const),
            pl.BlockSpec((1, 2 * width), const),
            pl.BlockSpec((1, width), const),
            pl.BlockSpec((1, width), const),
            pl.BlockSpec(w_sp.shape, lambda i: (0, 0, 0)),
            pl.BlockSpec((tm, width), const),
            pl.BlockSpec((width, d), const),
        ],
        out_specs=pl.BlockSpec((tm, d), lambda i: (i, 0)),
        compiler_params=_params(("arbitrary",)),
        name="sgu",
    )(x, g_pre.reshape(1, d), g_post.reshape(1, d), w_uv, b_uv.reshape(1, -1), g_ln.reshape(1, -1),
      b_ln.reshape(1, -1), w_sp, bias, w_out)


def kernel(x, g_norm, w_ffn_gate, w_ffn_up, w_ffn_down, w_in_ab, w_out_ab, lambda_params, g_subln,
           w_uv, b_uv, g_sgu_ln, b_sgu_ln, w_spatial, b_spatial, w_out_c):
    b, s, d = x.shape
    depth = g_norm.shape[0]
    assert w_in_ab.shape[2] == 3 * LANES * (DA_HEADS + SB_PAIRS)
    assert g_subln.shape[1] == LANES and w_spatial.shape[1:] == (SGU_GROUPS, SGU_CHUNK, SGU_CHUNK)
    cast = lambda w: w.astype(BF16)
    xf = x.reshape(b * s, d)
    for l in range(depth):
        g = g_norm[l]
        xf = _ffn(xf, g[0], g[1], cast(w_ffn_gate[l, 0]), cast(w_ffn_up[l, 0]), cast(w_ffn_down[l, 0]))
        if l % 2 == 0:
            e = l // 2
            lambda_init = 0.8 - 0.6 * math.exp(-0.3 * l)
            proj = _inproj(xf, g[2], cast(w_in_ab[e])).reshape(b, s, -1)
            oa = _diff_attn(proj, lambda_params[e], g_subln[e], lambda_init)
            ob = _sb_attn(proj)
            xf = _outproj(xf, oa.reshape(b * s, -1), ob.reshape(b * s, -1), cast(w_out_ab[e]), g[3])
        else:
            o = l // 2
            xf = _sgu(xf, g[2], g[3], cast(w_uv[o]), b_uv[o], g_sgu_ln[o], b_sgu_ln[o], w_spatial[o],
                      b_spatial[o], cast(w_out_c[o]))
        xf = _ffn(xf, g[4], g[5], cast(w_ffn_gate[l, 1]), cast(w_ffn_up[l, 1]), cast(w_ffn_down[l, 1]))
    return xf.reshape(b, s, d)
```

```python
import functools
import math

import jax
import jax.numpy as jnp
from jax import lax
from jax.experimental import pallas as pl
from jax.experimental.pallas import tpu as pltpu

F32 = jnp.float32
BF16 = jnp.bfloat16

EPS = 1e-6
LANES = 128
VMEM_LIMIT_BYTES = 52 * 1024 * 1024

DA_HEADS = 4
DA_HEAD_DIM = 64
SB_HEADS = 8
SB_HEAD_DIM = 64
SB_PAIRS = SB_HEADS * SB_HEAD_DIM // LANES
SGU_CHUNK = 128
SGU_GROUPS = 8
ATTN_TILE = 512
SB_BLOCK = 256
LOG2E = math.log2(math.e)
SB_DEAD = -160.0
POS_RADIX = 64


def _pick(n, prefs):
    for p in prefs:
        if n % p == 0:
            return p
    return n


def _rms(x, g):
    return x * lax.rsqrt(jnp.mean(x * x, axis=-1, keepdims=True) + EPS) * g


def _params(sem):
    return pltpu.CompilerParams(dimension_semantics=sem, vmem_limit_bytes=VMEM_LIMIT_BYTES)


def _ffn_kernel(x_ref, gpre_ref, gpost_ref, wg_ref, wu_ref, wd_ref, o_ref, h_ref, acc_ref):
    j = pl.program_id(1)

    @pl.when(j == 0)
    def _():
        h_ref[...] = _rms(x_ref[...], gpre_ref[...]).astype(BF16)
        acc_ref[...] = jnp.zeros_like(acc_ref)

    h = h_ref[...]
    gate = jnp.dot(h, wg_ref[...], preferred_element_type=F32)
    up = jnp.dot(h, wu_ref[...], preferred_element_type=F32)
    act = (gate * jax.nn.sigmoid(gate) * up).astype(BF16)
    acc_ref[...] += jnp.dot(act, wd_ref[...], preferred_element_type=F32)

    @pl.when(j == pl.num_programs(1) - 1)
    def _():
        o_ref[...] = x_ref[...] + 0.5 * _rms(acc_ref[...], gpost_ref[...])


def _ffn(x, g_pre, g_post, wg, wu, wd):
    n, d = x.shape
    f = wg.shape[1]
    tm = _pick(n, (512, 256, 128))
    tf = _pick(f, (1408, 512, 256, 128))
    return pl.pallas_call(
        _ffn_kernel,
        out_shape=jax.ShapeDtypeStruct((n, d), F32),
        grid=(n // tm, f // tf),
        in_specs=[
            pl.BlockSpec((tm, d), lambda i, j: (i, 0)),
            pl.BlockSpec((1, d), lambda i, j: (0, 0)),
            pl.BlockSpec((1, d), lambda i, j: (0, 0)),
            pl.BlockSpec((d, tf), lambda i, j: (0, j)),
            pl.BlockSpec((d, tf), lambda i, j: (0, j)),
            pl.BlockSpec((tf, d), lambda i, j: (j, 0)),
        ],
        out_specs=pl.BlockSpec((tm, d), lambda i, j: (i, 0)),
        scratch_shapes=[pltpu.VMEM((tm, d), BF16), pltpu.VMEM((tm, d), F32)],
        compiler_params=_params(("arbitrary", "arbitrary")),
        name="ffn",
    )(x, g_pre.reshape(1, d), g_post.reshape(1, d), wg, wu, wd)


def _inproj_kernel(x_ref, g_ref, w_ref, o_ref, h_ref):
    @pl.when(pl.program_id(1) == 0)
    def _():
        h_ref[...] = _rms(x_ref[...], g_ref[...]).astype(BF16)

    o_ref[...] = jnp.dot(h_ref[...], w_ref[...], preferred_element_type=F32).astype(o_ref.dtype)


def _inproj(x, g, w):
    n, d = x.shape
    c = w.shape[1]
    tm = _pick(n, (1024, 512, 256, 128))
    tn = _pick(c, (768, 512, 256, 128))
    return pl.pallas_call(
        _inproj_kernel,
        out_shape=jax.ShapeDtypeStruct((n, c), BF16),
        grid=(n // tm, c // tn),
        in_specs=[
            pl.BlockSpec((tm, d), lambda i, j: (i, 0)),
            pl.BlockSpec((1, d), lambda i, j: (0, 0)),
            pl.BlockSpec((d, tn), lambda i, j: (0, j)),
        ],
        out_specs=pl.BlockSpec((tm, tn), lambda i, j: (i, j)),
        scratch_shapes=[pltpu.VMEM((tm, d), BF16)],
        compiler_params=_params(("arbitrary", "arbitrary")),
        name="inproj",
    )(x, g.reshape(1, d), w)


def _split_pair(q):
    qf = q.astype(F32)
    lane = lax.broadcasted_iota(jnp.int32, qf.shape, 1)
    lo = jnp.where(lane < DA_HEAD_DIM, qf, 0.0).astype(BF16)
    hi = jnp.where(lane >= DA_HEAD_DIM, qf, 0.0).astype(BF16)
    return lo, hi


def _qk(q, k):
    return lax.dot_general(q, k, (((1,), (1,)), ((), ())), preferred_element_type=F32)


def _diff_kernel(q_ref, k_ref, v_ref, pos_ref, lp_ref, g_ref, o_ref, m_ref, l_ref, a_ref, *, t, lambda_init):
    h = pl.program_id(1)
    qi = pl.program_id(2)
    lane = lax.broadcasted_iota(jnp.int32, (t, LANES), 1)
    head_v = (jnp.zeros((t, LANES), jnp.int32) + (h + 1)).astype(F32)
    slope = jnp.exp2(-8.0 / DA_HEADS * head_v)
    t0 = (jnp.zeros((t, LANES), jnp.int32) + qi * t).astype(F32)
    aug = jnp.where(lane == 0, slope * POS_RADIX,
                    jnp.where(lane == 1, slope, jnp.where(lane == 2, -slope * t0, 0.0))).astype(BF16)
    q_aug = [jnp.concatenate([qm, aug], axis=1) for qm in _split_pair(q_ref[0])]
    ones = jnp.ones((t, LANES), BF16)
    row_i = lax.broadcasted_iota(jnp.int32, (t, t), 0)
    col_i = lax.broadcasted_iota(jnp.int32, (t, t), 1)

    def trip(kjs, mask=None):
        starts = [pl.multiple_of(kj * t, t) for kj in kjs]
        ks = [jnp.concatenate([k_ref[0, pl.ds(s0, t), :], pos_ref[pl.ds(s0, t), :]], axis=1) for s0 in starts]
        vs = [jnp.concatenate([v_ref[0, pl.ds(s0, t), :], ones], axis=1) for s0 in starts]
        sss = [[_qk(q_aug[mp], k) for k in ks] for mp in range(2)]
        if mask is not None:
            sss = [[jnp.where(mask, s, -jnp.inf) for s in ss] for ss in sss]
        m_news = []
        for mp, ss in enumerate(sss):
            mloc = functools.reduce(jnp.maximum, [jnp.max(s, axis=-1, keepdims=True) for s in ss])
            m_new = jnp.broadcast_to(mloc, (t, LANES))
            m_news.append(m_new if mask is not None else jnp.maximum(m_ref[mp], m_new))
        pvs = []
        for ss, m_new in zip(sss, m_news):
            m_wide = jnp.concatenate([m_new] * (t // LANES), axis=1)
            pvs.append(sum(jnp.dot(jnp.exp(s - m_wide).astype(BF16), v, preferred_element_type=F32)
                           for s, v in zip(ss, vs)))
        for mp, (pv, m_new) in enumerate(zip(pvs, m_news)):
            if mask is None:
                alpha = jnp.exp(m_ref[mp] - m_new)
                a_ref[mp] = alpha * a_ref[mp] + pv[:, :LANES]
                l_ref[mp] = alpha * l_ref[mp] + pv[:, LANES:]
            else:
                a_ref[mp] = pv[:, :LANES]
                l_ref[mp] = pv[:, LANES:]
            m_ref[mp] = m_new

    trip([qi], col_i <= row_i)

    def body(i, c):
        trip([2 * i, 2 * i + 1])
        return c

    lax.fori_loop(0, qi // 2, body, 0)

    @pl.when(qi % 2 == 1)
    def _():
        trip([qi - 1])

    lp = lp_ref[...]
    lam = (jnp.exp(jnp.sum(lp[0:1] * lp[1:2], axis=-1, keepdims=True))
           - jnp.exp(jnp.sum(lp[2:3] * lp[3:4], axis=-1, keepdims=True)) + lambda_init)
    oa = a_ref[0] / l_ref[0] - lam * (a_ref[1] / l_ref[1])
    o_ref[0] = (_rms(oa, g_ref[...]) * (1.0 - lambda_init)).astype(o_ref.dtype)


def _key_positions(s):
    pos = jnp.arange(s, dtype=jnp.int32)[:, None]
    lane = jnp.arange(LANES, dtype=jnp.int32)[None, :]
    tab = jnp.where(lane == 0, pos // POS_RADIX, jnp.where(lane == 1, pos % POS_RADIX, (lane == 2).astype(jnp.int32)))
    return tab.astype(BF16)


def _diff_attn(proj, lam_p, g_subln, lambda_init):
    b, s, _ = proj.shape
    t = _pick(s, (ATTN_TILE, LANES))
    assert s // POS_RADIX <= 256 and s // t <= 256
    kern = functools.partial(_diff_kernel, t=t, lambda_init=lambda_init)
    return pl.pallas_call(
        kern,
        out_shape=jax.ShapeDtypeStruct((b, s, DA_HEADS * LANES), BF16),
        grid=(b, DA_HEADS, s // t),
        in_specs=[
            pl.BlockSpec((1, t, LANES), lambda bi, h, qi: (bi, qi, h)),
            pl.BlockSpec((1, s, LANES), lambda bi, h, qi: (bi, 0, DA_HEADS + h)),
            pl.BlockSpec((1, s, LANES), lambda bi, h, qi: (bi, 0, 2 * DA_HEADS + h)),
            pl.BlockSpec((s, LANES), lambda bi, h, qi: (0, 0)),
            pl.BlockSpec(lam_p.shape, lambda bi, h, qi: (0, 0)),
            pl.BlockSpec((1, LANES), lambda bi, h, qi: (0, 0)),
        ],
        out_specs=pl.BlockSpec((1, t, LANES), lambda bi, h, qi: (bi, qi, h)),
        scratch_shapes=[pltpu.VMEM((2, t, LANES), F32)] * 3,
        compiler_params=_params(("arbitrary", "arbitrary", "arbitrary")),
        name="diff_attn",
    )(proj, proj, proj, _key_positions(s), lam_p, g_subln.reshape(1, LANES))


def _sb_kernel(q_ref, k_ref, v_ref, w_ref, o_ref, c_ref, a_ref, *, tq, tk):
    qi = pl.program_id(2)
    ratio = tq // tk
    nsub = tk // SB_BLOCK
    qs = _split_pair(q_ref[0])
    row_i = lax.broadcasted_iota(jnp.int32, (tq, SB_BLOCK), 0)
    col_i = lax.broadcasted_iota(jnp.int32, (tq, SB_BLOCK), 1)
    sign = jnp.int32(-2 ** 31)

    def heavy(kjs, diag=None):
        starts = [pl.multiple_of(kj * tk, tk) for kj in kjs]
        ks = [k_ref[0, pl.ds(s0, tk), :] for s0 in starts]
        vs = [v_ref[0, pl.ds(s0, tk), :] for s0 in starts]
        wmat = w_ref[...]
        chains = [(ti, hh) for ti in range(len(kjs)) for hh in range(2)]
        zs = [_qk(qs[hh], ks[ti]) for ti, hh in chains]
        later = [None] * len(chains)
        wts = [[None] * nsub for _ in chains]
        for sb in reversed(range(nsub)):
            zbs = [z[:, sb * SB_BLOCK:(sb + 1) * SB_BLOCK] for z in zs]
            strict = None if diag is None else col_i + (diag + sb * SB_BLOCK) < row_i
            sps = []
            for zb in zbs:
                neg_abs = lax.bitcast_convert_type(lax.bitcast_convert_type(zb, jnp.int32) | sign, F32)
                sp = jnp.maximum(zb, 0.0) + jnp.log(1.0 + jnp.exp2(neg_abs)) * LOG2E
                sps.append((sp if strict is None else jnp.where(strict, sp, 0.0)).astype(BF16))
            ces = [jnp.dot(sp, wmat, preferred_element_type=F32) for sp in sps]
            for c, (zb, ce) in enumerate(zip(zbs, ces)):
                arg = zb + ce
                if later[c] is not None:
                    arg = arg + jnp.concatenate([later[c]] * (SB_BLOCK // LANES), axis=1)
                w = jnp.exp2(arg)
                wts[c][sb] = (w if strict is None else jnp.where(strict, w, 0.0)).astype(BF16)
                total = jnp.broadcast_to(ce[:, 0:1], (tq, LANES))
                later[c] = total if later[c] is None else later[c] + total
        pvs = [jnp.dot(wts[c][0] if nsub == 1 else jnp.concatenate(wts[c], axis=1), vs[ti],
                       preferred_element_type=F32) for c, (ti, hh) in enumerate(chains)]
        return list(zip(pvs, later))

    def combine(outs):
        for c, (pv, total) in enumerate(outs):
            hh = c % 2
            carry = c_ref[hh]
            a_ref[hh] += jnp.exp2(carry) * pv
            c_ref[hh] = carry + total

    for hh, (pv, total) in enumerate(heavy([qi * ratio + ratio - 1], (ratio - 1) * tk)):
        a_ref[hh] = pv
        c_ref[hh] = total
    for i in reversed(range(ratio - 1)):
        combine(heavy([qi * ratio + i], i * tk))

    def alive():
        return jnp.max(jnp.maximum(c_ref[0], c_ref[1])) > SB_DEAD

    n = qi * ratio

    @pl.when(jnp.logical_and(n >= 1, alive()))
    def _():
        combine(heavy([n - 1]))

    m = jnp.maximum(n - 1, 0)

    def body(state):
        i, _ = state
        combine(heavy([m - 1 - 2 * i, m - 2 - 2 * i]))
        return i + 1, alive()

    _, live = lax.while_loop(lambda st: jnp.logical_and(st[0] < m // 2, st[1]), body, (0, alive()))

    @pl.when(jnp.logical_and(m % 2 == 1, live))
    def _():
        combine(heavy([0]))

    lane = lax.broadcasted_iota(jnp.int32, (tq, LANES), 1)
    o_ref[0] = jnp.where(lane < SB_HEAD_DIM, a_ref[0], a_ref[1]).astype(o_ref.dtype)


def _suffix_matrix():
    j = jnp.arange(SB_BLOCK)[:, None]
    s = jnp.arange(SB_BLOCK)[None, :]
    return -(j >= s).astype(BF16)


def _sb_attn(proj):
    b, s, _ = proj.shape
    t = _pick(s, (ATTN_TILE, SB_BLOCK))
    tk = t
    base = 3 * DA_HEADS
    kern = functools.partial(_sb_kernel, tq=t, tk=tk)
    return pl.pallas_call(
        kern,
        out_shape=jax.ShapeDtypeStruct((b, s, SB_PAIRS * LANES), BF16),
        grid=(b, SB_PAIRS, s // t),
        in_specs=[
            pl.BlockSpec((1, t, LANES), lambda bi, p, qi: (bi, qi, base + p)),
            pl.BlockSpec((1, s, LANES), lambda bi, p, qi: (bi, 0, base + SB_PAIRS + p)),
            pl.BlockSpec((1, s, LANES), lambda bi, p, qi: (bi, 0, base + 2 * SB_PAIRS + p)),
            pl.BlockSpec((SB_BLOCK, SB_BLOCK), lambda bi, p, qi: (0, 0)),
        ],
        out_specs=pl.BlockSpec((1, t, LANES), lambda bi, p, qi: (bi, qi, p)),
        scratch_shapes=[pltpu.VMEM((2, t, LANES), F32)] * 2,
        compiler_params=_params(("arbitrary", "arbitrary", "arbitrary")),
        name="sb_attn",
    )(proj, proj, proj, _suffix_matrix())


def _outproj_kernel(x_ref, oa_ref, ob_ref, wa_ref, wb_ref, g_ref, o_ref):
    m = (jnp.dot(oa_ref[...], wa_ref[...], preferred_element_type=F32)
         + jnp.dot(ob_ref[...], wb_ref[...], preferred_element_type=F32))
    o_ref[...] = x_ref[...] + _rms(m, g_ref[...])


def _outproj(x, oa, ob, w, g):
    n, d = x.shape
    ca, cb = oa.shape[1], ob.shape[1]
    tm = _pick(n, (1024, 512, 256, 128))
    return pl.pallas_call(
        _outproj_kernel,
        out_shape=jax.ShapeDtypeStruct((n, d), F32),
        grid=(n // tm,),
        in_specs=[
            pl.BlockSpec((tm, d), lambda i: (i, 0)),
            pl.BlockSpec((tm, ca), lambda i: (i, 0)),
            pl.BlockSpec((tm, cb), lambda i: (i, 0)),
            pl.BlockSpec((ca, d), lambda i: (0, 0)),
            pl.BlockSpec((cb, d), lambda i: (0, 0)),
            pl.BlockSpec((1, d), lambda i: (0, 0)),
        ],
        out_specs=pl.BlockSpec((tm, d), lambda i: (i, 0)),
        compiler_params=_params(("arbitrary",)),
        name="outproj",
    )(x, oa, ob, w[:ca], w[ca:], g.reshape(1, d))


def _sgu_kernel(x_ref, gpre_ref, gpost_ref, wuv_ref, buv_ref, gln_ref, bln_ref, wsp_ref, bsp_ref, wo_ref, o_ref,
                *, tm, width):
    x = x_ref[...]
    h = _rms(x, gpre_ref[...]).astype(BF16)
    zz = jax.nn.gelu(jnp.dot(h, wuv_ref[...], preferred_element_type=F32) + buv_ref[...])
    u = zz[:, :width]
    v = zz[:, width:]
    mu = jnp.mean(v, axis=-1, keepdims=True)
    vc = v - mu
    v = (vc * lax.rsqrt(jnp.mean(vc * vc, axis=-1, keepdims=True) + EPS) * gln_ref[...] + bln_ref[...]).astype(BF16)

    t_i = lax.broadcasted_iota(jnp.int32, (SGU_CHUNK, SGU_CHUNK), 0)
    s_i = lax.broadcasted_iota(jnp.int32, (SGU_CHUNK, SGU_CHUNK), 1)
    gch = width // SGU_GROUPS
    cols = []
    for g in range(SGU_GROUPS):
        wg = jnp.where(s_i <= t_i, wsp_ref[g], 0.0).astype(BF16)
        rows = [jnp.dot(wg, v[n * SGU_CHUNK:(n + 1) * SGU_CHUNK, g * gch:(g + 1) * gch],
                        preferred_element_type=F32) for n in range(tm // SGU_CHUNK)]
        cols.append(jnp.concatenate(rows, axis=0))
    mixed = jnp.concatenate(cols, axis=1) + bsp_ref[...]
    y = jnp.dot((u * mixed).astype(BF16), wo_ref[...], preferred_element_type=F32)
    o_ref[...] = x + _rms(y, gpost_ref[...])


def _sgu(x, g_pre, g_post, w_uv, b_uv, g_ln, b_ln, w_sp, b_sp, w_out):
    n, d = x.shape
    width = w_out.shape[0]
    tm = _pick(n, (512, 256, 128))
    gch = width // SGU_GROUPS
    bias = jnp.tile(jnp.repeat(b_sp.T, gch, axis=1), (tm // SGU_CHUNK, 1))
    kern = functools.partial(_sgu_kernel, tm=tm, width=width)
    const = lambda i: (0, 0)
    return pl.pallas_call(
        kern,
        out_shape=jax.ShapeDtypeStruct((n, d), F32),
        grid=(n // tm,),
        in_specs=[
            pl.BlockSpec((tm, d), lambda i: (i, 0)),
            pl.BlockSpec((1, d), const),
            pl.BlockSpec((1, d), const),
            pl.BlockSpec((d, 2 * width), const),
            pl.BlockSpec((1, 2 * width), const),
            pl.BlockSpec((1, width), const),
            pl.BlockSpec((1, width), const),
            pl.BlockSpec(w_sp.shape, lambda i: (0, 0, 0)),
            pl.BlockSpec((tm, width), const),
            pl.BlockSpec((width, d), const),
        ],
        out_specs=pl.BlockSpec((tm, d), lambda i: (i, 0)),
        compiler_params=_params(("arbitrary",)),
        name="sgu",
    )(x, g_pre.reshape(1, d), g_post.reshape(1, d), w_uv, b_uv.reshape(1, -1), g_ln.reshape(1, -1),
      b_ln.reshape(1, -1), w_sp, bias, w_out)


def _q_scales():
    da_w, sb_w = DA_HEADS * LANES, SB_PAIRS * LANES
    one = lambda n: jnp.ones((n,), F32)
    return jnp.concatenate([one(da_w) * DA_HEAD_DIM ** -0.5, one(2 * da_w),
                            one(sb_w) * (LOG2E * SB_HEAD_DIM ** -0.5), one(2 * sb_w)])


def kernel(x, g_norm, w_ffn_gate, w_ffn_up, w_ffn_down, w_in_ab, w_out_ab, lambda_params, g_subln,
           w_uv, b_uv, g_sgu_ln, b_sgu_ln, w_spatial, b_spatial, w_out_c):
    b, s, d = x.shape
    depth = g_norm.shape[0]
    assert w_in_ab.shape[2] == 3 * LANES * (DA_HEADS + SB_PAIRS)
    assert g_subln.shape[1] == LANES and w_spatial.shape[1:] == (SGU_GROUPS, SGU_CHUNK, SGU_CHUNK)
    cast = lambda w: w.astype(BF16)
    xf = x.reshape(b * s, d)
    for l in range(depth):
        g = g_norm[l]
        xf = _ffn(xf, g[0], g[1], cast(w_ffn_gate[l, 0]), cast(w_ffn_up[l, 0]), cast(w_ffn_down[l, 0]))
        if l % 2 == 0:
            e = l // 2
            lambda_init = 0.8 - 0.6 * math.exp(-0.3 * l)
            proj = _inproj(xf, g[2], cast(w_in_ab[e] * _q_scales())).reshape(b, s, -1)
            oa = _diff_attn(proj, lambda_params[e], g_subln[e], lambda_init)
            ob = _sb_attn(proj)
            xf = _outproj(xf, oa.reshape(b * s, -1), ob.reshape(b * s, -1), cast(w_out_ab[e]), g[3])
        else:
            o = l // 2
            xf = _sgu(xf, g[2], g[3], cast(w_uv[o]), b_uv[o], g_sgu_ln[o], b_sgu_ln[o], w_spatial[o],
                      b_spatial[o], cast(w_out_c[o]))
        xf = _ffn(xf, g[4], g[5], cast(w_ffn_gate[l, 1]), cast(w_ffn_up[l, 1]), cast(w_ffn_down[l, 1]))
    return xf.reshape(b, s, d)
```

```python
import functools
import math

import jax
import jax.numpy as jnp
from jax import lax
from jax.experimental import pallas as pl
from jax.experimental.pallas import tpu as pltpu

F32 = jnp.float32
BF16 = jnp.bfloat16

EPS = 1e-6
LANES = 128
VMEM_LIMIT_BYTES = 52 * 1024 * 1024

DA_HEADS = 4
DA_HEAD_DIM = 64
SB_HEADS = 8
SB_HEAD_DIM = 64
SB_PAIRS = SB_HEADS * SB_HEAD_DIM // LANES
SGU_CHUNK = 128
SGU_GROUPS = 8
ATTN_TILE = 512
SB_BLOCK = 256
LOG2E = math.log2(math.e)
SB_DEAD = -160.0
DIFF_DEAD = 110.0
POS_RADIX = 64


def _pick(n, prefs):
    for p in prefs:
        if n % p == 0:
            return p
    return n


def _rms(x, g):
    return x * lax.rsqrt(jnp.mean(x * x, axis=-1, keepdims=True) + EPS) * g


def _params(sem):
    return pltpu.CompilerParams(dimension_semantics=sem, vmem_limit_bytes=VMEM_LIMIT_BYTES)


def _ffn_kernel(x_ref, gpre_ref, gpost_ref, wg_ref, wu_ref, wd_ref, o_ref, h_ref, acc_ref):
    j = pl.program_id(1)

    @pl.when(j == 0)
    def _():
        h_ref[...] = _rms(x_ref[...], gpre_ref[...]).astype(BF16)
        acc_ref[...] = jnp.zeros_like(acc_ref)

    h = h_ref[...]
    gate = jnp.dot(h, wg_ref[...], preferred_element_type=F32)
    up = jnp.dot(h, wu_ref[...], preferred_element_type=F32)
    act = (gate * jax.nn.sigmoid(gate) * up).astype(BF16)
    acc_ref[...] += jnp.dot(act, wd_ref[...], preferred_element_type=F32)

    @pl.when(j == pl.num_programs(1) - 1)
    def _():
        o_ref[...] = x_ref[...] + 0.5 * _rms(acc_ref[...], gpost_ref[...])


def _ffn(x, g_pre, g_post, wg, wu, wd):
    n, d = x.shape
    f = wg.shape[1]
    tm = _pick(n, (512, 256, 128))
    tf = _pick(f, (1408, 512, 256, 128))
    return pl.pallas_call(
        _ffn_kernel,
        out_shape=jax.ShapeDtypeStruct((n, d), F32),
        grid=(n // tm, f // tf),
        in_specs=[
            pl.BlockSpec((tm, d), lambda i, j: (i, 0)),
            pl.BlockSpec((1, d), lambda i, j: (0, 0)),
            pl.BlockSpec((1, d), lambda i, j: (0, 0)),
            pl.BlockSpec((d, tf), lambda i, j: (0, j)),
            pl.BlockSpec((d, tf), lambda i, j: (0, j)),
            pl.BlockSpec((tf, d), lambda i, j: (j, 0)),
        ],
        out_specs=pl.BlockSpec((tm, d), lambda i, j: (i, 0)),
        scratch_shapes=[pltpu.VMEM((tm, d), BF16), pltpu.VMEM((tm, d), F32)],
        compiler_params=_params(("arbitrary", "arbitrary")),
        name="ffn",
    )(x, g_pre.reshape(1, d), g_post.reshape(1, d), wg, wu, wd)


def _inproj_kernel(x_ref, g_ref, w_ref, o_ref, h_ref):
    @pl.when(pl.program_id(1) == 0)
    def _():
        h_ref[...] = _rms(x_ref[...], g_ref[...]).astype(BF16)

    o_ref[...] = jnp.dot(h_ref[...], w_ref[...], preferred_element_type=F32).astype(o_ref.dtype)


def _inproj(x, g, w):
    n, d = x.shape
    c = w.shape[1]
    tm = _pick(n, (1024, 512, 256, 128))
    tn = _pick(c, (768, 512, 256, 128))
    return pl.pallas_call(
        _inproj_kernel,
        out_shape=jax.ShapeDtypeStruct((n, c), BF16),
        grid=(n // tm, c // tn),
        in_specs=[
            pl.BlockSpec((tm, d), lambda i, j: (i, 0)),
            pl.BlockSpec((1, d), lambda i, j: (0, 0)),
            pl.BlockSpec((d, tn), lambda i, j: (0, j)),
        ],
        out_specs=pl.BlockSpec((tm, tn), lambda i, j: (i, j)),
        scratch_shapes=[pltpu.VMEM((tm, d), BF16)],
        compiler_params=_params(("arbitrary", "arbitrary")),
        name="inproj",
    )(x, g.reshape(1, d), w)


def _split_pair(q):
    qf = q.astype(F32)
    lane = lax.broadcasted_iota(jnp.int32, qf.shape, 1)
    lo = jnp.where(lane < DA_HEAD_DIM, qf, 0.0).astype(BF16)
    hi = jnp.where(lane >= DA_HEAD_DIM, qf, 0.0).astype(BF16)
    return lo, hi


def _qk(q, k):
    return lax.dot_general(q, k, (((1,), (1,)), ((), ())), preferred_element_type=F32)


def _diff_kernel(q_ref, k_ref, v_ref, pos_ref, lp_ref, g_ref, o_ref, m_ref, l_ref, a_ref, kmax_ref, *, t,
                 lambda_init):
    h = pl.program_id(1)
    qi = pl.program_id(2)
    lane = lax.broadcasted_iota(jnp.int32, (t, LANES), 1)
    head_v = (jnp.zeros((t, LANES), jnp.int32) + (h + 1)).astype(F32)
    slope = jnp.exp2(-8.0 / DA_HEADS * head_v)
    t0 = (jnp.zeros((t, LANES), jnp.int32) + qi * t).astype(F32)
    aug = jnp.where(lane == 0, slope * POS_RADIX,
                    jnp.where(lane == 1, slope, jnp.where(lane == 2, -slope * t0, 0.0))).astype(BF16)
    q_aug = [jnp.concatenate([qm, aug], axis=1) for qm in _split_pair(q_ref[0])]
    ones = jnp.ones((t, LANES), BF16)
    row_i = lax.broadcasted_iota(jnp.int32, (t, t), 0)
    col_i = lax.broadcasted_iota(jnp.int32, (t, t), 1)

    def trip(kjs, mask=None):
        starts = [pl.multiple_of(kj * t, t) for kj in kjs]
        ks = [jnp.concatenate([k_ref[0, pl.ds(s0, t), :], pos_ref[pl.ds(s0, t), :]], axis=1) for s0 in starts]
        vs = [jnp.concatenate([v_ref[0, pl.ds(s0, t), :], ones], axis=1) for s0 in starts]
        sss = [[_qk(q_aug[mp], k) for k in ks] for mp in range(2)]
        if mask is not None:
            sss = [[jnp.where(mask, s, -jnp.inf) for s in ss] for ss in sss]
        m_news = []
        for mp, ss in enumerate(sss):
            mloc = functools.reduce(jnp.maximum, [jnp.max(s, axis=-1, keepdims=True) for s in ss])
            m_new = jnp.broadcast_to(mloc, (t, LANES))
            m_news.append(m_new if mask is not None else jnp.maximum(m_ref[mp], m_new))
        pvs = []
        for ss, m_new in zip(sss, m_news):
            m_wide = jnp.concatenate([m_new] * (t // LANES), axis=1)
            pvs.append(sum(jnp.dot(jnp.exp(s - m_wide).astype(BF16), v, preferred_element_type=F32)
                           for s, v in zip(ss, vs)))
        for mp, (pv, m_new) in enumerate(zip(pvs, m_news)):
            if mask is None:
                alpha = jnp.exp(m_ref[mp] - m_new)
                a_ref[mp] = alpha * a_ref[mp] + pv[:, :LANES]
                l_ref[mp] = alpha * l_ref[mp] + pv[:, LANES:]
            else:
                a_ref[mp] = pv[:, :LANES]
                l_ref[mp] = pv[:, LANES:]
            m_ref[mp] = m_new

    trip([qi], col_i <= row_i)

    @pl.when(qi == 0)
    def _():
        ksq = [jnp.sum(jnp.square(k_ref[0, c * t:(c + 1) * t, :].astype(F32)), axis=-1, keepdims=True)
               for c in range(k_ref.shape[1] // t)]
        kmax_ref[0] = jnp.sqrt(jnp.max(functools.reduce(jnp.maximum, ksq)))

    qn = jnp.sqrt(jnp.sum(jnp.square(q_ref[0].astype(F32)), axis=-1, keepdims=True))
    reach = jnp.broadcast_to(qn * kmax_ref[0] + DIFF_DEAD, (t, LANES))

    def alive(kj):
        dist = (jnp.zeros((t, LANES), jnp.int32) + (qi - 1 - kj) * t).astype(F32)
        live = jnp.logical_or(reach - m_ref[0] > slope * dist, reach - m_ref[1] > slope * dist)
        return jnp.max(jnp.where(live, 1, 0)) > 0

    def body(state):
        i, _ = state
        kj = qi - 1 - 2 * i
        nxt = alive(kj - 2)
        trip([kj, kj - 1])
        return i + 1, nxt

    _, live = lax.while_loop(lambda st: jnp.logical_and(st[0] < qi // 2, st[1]), body, (0, alive(qi - 1)))

    @pl.when(jnp.logical_and(qi % 2 == 1, live))
    def _():
        trip([0])

    lp = lp_ref[...]
    lam = (jnp.exp(jnp.sum(lp[0:1] * lp[1:2], axis=-1, keepdims=True))
           - jnp.exp(jnp.sum(lp[2:3] * lp[3:4], axis=-1, keepdims=True)) + lambda_init)
    oa = a_ref[0] / l_ref[0] - lam * (a_ref[1] / l_ref[1])
    o_ref[0] = (_rms(oa, g_ref[...]) * (1.0 - lambda_init)).astype(o_ref.dtype)


def _key_positions(s):
    pos = jnp.arange(s, dtype=jnp.int32)[:, None]
    lane = jnp.arange(LANES, dtype=jnp.int32)[None, :]
    tab = jnp.where(lane == 0, pos // POS_RADIX, jnp.where(lane == 1, pos % POS_RADIX, (lane == 2).astype(jnp.int32)))
    return tab.astype(BF16)


def _diff_attn(proj, lam_p, g_subln, lambda_init):
    b, s, _ = proj.shape
    t = _pick(s, (ATTN_TILE, LANES))
    assert s // POS_RADIX <= 256 and s // t <= 256
    kern = functools.partial(_diff_kernel, t=t, lambda_init=lambda_init)
    return pl.pallas_call(
        kern,
        out_shape=jax.ShapeDtypeStruct((b, s, DA_HEADS * LANES), BF16),
        grid=(b, DA_HEADS, s // t),
        in_specs=[
            pl.BlockSpec((1, t, LANES), lambda bi, h, qi: (bi, qi, h)),
            pl.BlockSpec((1, s, LANES), lambda bi, h, qi: (bi, 0, DA_HEADS + h)),
            pl.BlockSpec((1, s, LANES), lambda bi, h, qi: (bi, 0, 2 * DA_HEADS + h)),
            pl.BlockSpec((s, LANES), lambda bi, h, qi: (0, 0)),
            pl.BlockSpec(lam_p.shape, lambda bi, h, qi: (0, 0)),
            pl.BlockSpec((1, LANES), lambda bi, h, qi: (0, 0)),
        ],
        out_specs=pl.BlockSpec((1, t, LANES), lambda bi, h, qi: (bi, qi, h)),
        scratch_shapes=[pltpu.VMEM((2, t, LANES), F32)] * 3 + [pltpu.SMEM((1,), F32)],
        compiler_params=_params(("arbitrary", "arbitrary", "arbitrary")),
        name="diff_attn",
    )(proj, proj, proj, _key_positions(s), lam_p, g_subln.reshape(1, LANES))


def _sb_kernel(q_ref, k_ref, v_ref, w_ref, o_ref, c_ref, a_ref, *, tq, tk):
    qi = pl.program_id(2)
    ratio = tq // tk
    nsub = tk // SB_BLOCK
    qs = _split_pair(q_ref[0])
    row_i = lax.broadcasted_iota(jnp.int32, (tq, SB_BLOCK), 0)
    col_i = lax.broadcasted_iota(jnp.int32, (tq, SB_BLOCK), 1)
    sign = jnp.int32(-2 ** 31)

    def heavy(kjs, diag=None):
        starts = [pl.multiple_of(kj * tk, tk) for kj in kjs]
        ks = [k_ref[0, pl.ds(s0, tk), :] for s0 in starts]
        vs = [v_ref[0, pl.ds(s0, tk), :] for s0 in starts]
        wmat = w_ref[...]
        chains = [(ti, hh) for ti in range(len(kjs)) for hh in range(2)]
        zs = [_qk(qs[hh], ks[ti]) for ti, hh in chains]
        later = [None] * len(chains)
        wts = [[None] * nsub for _ in chains]
        for sb in reversed(range(nsub)):
            zbs = [z[:, sb * SB_BLOCK:(sb + 1) * SB_BLOCK] for z in zs]
            strict = None if diag is None else col_i + (diag + sb * SB_BLOCK) < row_i
            sps = []
            for zb in zbs:
                neg_abs = lax.bitcast_convert_type(lax.bitcast_convert_type(zb, jnp.int32) | sign, F32)
                sp = jnp.maximum(zb, 0.0) + jnp.log(1.0 + jnp.exp2(neg_abs)) * LOG2E
                sps.append((sp if strict is None else jnp.where(strict, sp, 0.0)).astype(BF16))
            ces = [jnp.dot(sp, wmat, preferred_element_type=F32) for sp in sps]
            for c, (zb, ce) in enumerate(zip(zbs, ces)):
                arg = zb + ce
                if later[c] is not None:
                    arg = arg + jnp.concatenate([later[c]] * (SB_BLOCK // LANES), axis=1)
                w = jnp.exp2(arg)
                wts[c][sb] = (w if strict is None else jnp.where(strict, w, 0.0)).astype(BF16)
                total = jnp.broadcast_to(ce[:, 0:1], (tq, LANES))
                later[c] = total if later[c] is None else later[c] + total
        pvs = [jnp.dot(wts[c][0] if nsub == 1 else jnp.concatenate(wts[c], axis=1), vs[ti],
                       preferred_element_type=F32) for c, (ti, hh) in enumerate(chains)]
        return list(zip(pvs, later))

    def combine(outs):
        for c, (pv, total) in enumerate(outs):
            hh = c % 2
            carry = c_ref[hh]
            a_ref[hh] += jnp.exp2(carry) * pv
            c_ref[hh] = carry + total

    for hh, (pv, total) in enumerate(heavy([qi * ratio + ratio - 1], (ratio - 1) * tk)):
        a_ref[hh] = pv
        c_ref[hh] = total
    for i in reversed(range(ratio - 1)):
        combine(heavy([qi * ratio + i], i * tk))

    def alive():
        return jnp.max(jnp.maximum(c_ref[0], c_ref[1])) > SB_DEAD

    n = qi * ratio

    @pl.when(jnp.logical_and(n >= 1, alive()))
    def _():
        combine(heavy([n - 1]))

    m = jnp.maximum(n - 1, 0)

    def body(state):
        i, _ = state
        combine(heavy([m - 1 - 2 * i, m - 2 - 2 * i]))
        return i + 1, alive()

    _, live = lax.while_loop(lambda st: jnp.logical_and(st[0] < m // 2, st[1]), body, (0, alive()))

    @pl.when(jnp.logical_and(m % 2 == 1, live))
    def _():
        combine(heavy([0]))

    lane = lax.broadcasted_iota(jnp.int32, (tq, LANES), 1)
    o_ref[0] = jnp.where(lane < SB_HEAD_DIM, a_ref[0], a_ref[1]).astype(o_ref.dtype)


def _suffix_matrix():
    j = jnp.arange(SB_BLOCK)[:, None]
    s = jnp.arange(SB_BLOCK)[None, :]
    return -(j >= s).astype(BF16)


def _sb_attn(proj):
    b, s, _ = proj.shape
    t = _pick(s, (ATTN_TILE, SB_BLOCK))
    tk = t
    base = 3 * DA_HEADS
    kern = functools.partial(_sb_kernel, tq=t, tk=tk)
    return pl.pallas_call(
        kern,
        out_shape=jax.ShapeDtypeStruct((b, s, SB_PAIRS * LANES), BF16),
        grid=(b, SB_PAIRS, s // t),
        in_specs=[
            pl.BlockSpec((1, t, LANES), lambda bi, p, qi: (bi, qi, base + p)),
            pl.BlockSpec((1, s, LANES), lambda bi, p, qi: (bi, 0, base + SB_PAIRS + p)),
            pl.BlockSpec((1, s, LANES), lambda bi, p, qi: (bi, 0, base + 2 * SB_PAIRS + p)),
            pl.BlockSpec((SB_BLOCK, SB_BLOCK), lambda bi, p, qi: (0, 0)),
        ],
        out_specs=pl.BlockSpec((1, t, LANES), lambda bi, p, qi: (bi, qi, p)),
        scratch_shapes=[pltpu.VMEM((2, t, LANES), F32)] * 2,
        compiler_params=_params(("arbitrary", "arbitrary", "arbitrary")),
        name="sb_attn",
    )(proj, proj, proj, _suffix_matrix())


def _outproj_kernel(x_ref, oa_ref, ob_ref, wa_ref, wb_ref, g_ref, o_ref):
    m = (jnp.dot(oa_ref[...], wa_ref[...], preferred_element_type=F32)
         + jnp.dot(ob_ref[...], wb_ref[...], preferred_element_type=F32))
    o_ref[...] = x_ref[...] + _rms(m, g_ref[...])


def _outproj(x, oa, ob, w, g):
    n, d = x.shape
    ca, cb = oa.shape[1], ob.shape[1]
    tm = _pick(n, (1024, 512, 256, 128))
    return pl.pallas_call(
        _outproj_kernel,
        out_shape=jax.ShapeDtypeStruct((n, d), F32),
        grid=(n // tm,),
        in_specs=[
            pl.BlockSpec((tm, d), lambda i: (i, 0)),
            pl.BlockSpec((tm, ca), lambda i: (i, 0)),
            pl.BlockSpec((tm, cb), lambda i: (i, 0)),
            pl.BlockSpec((ca, d), lambda i: (0, 0)),
            pl.BlockSpec((cb, d), lambda i: (0, 0)),
            pl.BlockSpec((1, d), lambda i: (0, 0)),
        ],
        out_specs=pl.BlockSpec((tm, d), lambda i: (i, 0)),
        compiler_params=_params(("arbitrary",)),
        name="outproj",
    )(x, oa, ob, w[:ca], w[ca:], g.reshape(1, d))


def _sgu_kernel(x_ref, gpre_ref, gpost_ref, wuv_ref, buv_ref, gln_ref, bln_ref, wsp_ref, bsp_ref, wo_ref, o_ref,
                *, tm, width):
    x = x_ref[...]
    h = _rms(x, gpre_ref[...]).astype(BF16)
    zz = jax.nn.gelu(jnp.dot(h, wuv_ref[...], preferred_element_type=F32) + buv_ref[...])
    u = zz[:, :width]
    v = zz[:, width:]
    mu = jnp.mean(v, axis=-1, keepdims=True)
    vc = v - mu
    v = (vc * lax.rsqrt(jnp.mean(vc * vc, axis=-1, keepdims=True) + EPS) * gln_ref[...] + bln_ref[...]).astype(BF16)

    t_i = lax.broadcasted_iota(jnp.int32, (SGU_CHUNK, SGU_CHUNK), 0)
    s_i = lax.broadcasted_iota(jnp.int32, (SGU_CHUNK, SGU_CHUNK), 1)
    gch = width // SGU_GROUPS
    cols = []
    for g in range(SGU_GROUPS):
        wg = jnp.where(s_i <= t_i, wsp_ref[g], 0.0).astype(BF16)
        rows = [jnp.dot(wg, v[n * SGU_CHUNK:(n + 1) * SGU_CHUNK, g * gch:(g + 1) * gch],
                        preferred_element_type=F32) for n in range(tm // SGU_CHUNK)]
        cols.append(jnp.concatenate(rows, axis=0))
    mixed = jnp.concatenate(cols, axis=1) + bsp_ref[...]
    y = jnp.dot((u * mixed).astype(BF16), wo_ref[...], preferred_element_type=F32)
    o_ref[...] = x + _rms(y, gpost_ref[...])


def _sgu(x, g_pre, g_post, w_uv, b_uv, g_ln, b_ln, w_sp, b_sp, w_out):
    n, d = x.shape
    width = w_out.shape[0]
    tm = _pick(n, (512, 256, 128))
    gch = width // SGU_GROUPS
    bias = jnp.tile(jnp.repeat(b_sp.T, gch, axis=1), (tm // SGU_CHUNK, 1))
    kern = functools.partial(_sgu_kernel, tm=tm, width=width)
    const = lambda i: (0, 0)
    return pl.pallas_call(
        kern,
        out_shape=jax.ShapeDtypeStruct((n, d), F32),
        grid=(n // tm,),
        in_specs=[
            pl.BlockSpec((tm, d), lambda i: (i, 0)),
            pl.BlockSpec((1, d), const),
            pl.BlockSpec((1, d), const),
            pl.BlockSpec((d, 2 * width), const),
            pl.BlockSpec((1, 2 * width), const),
            pl.BlockSpec((1, width), const),
            pl.BlockSpec((1, width), const),
            pl.BlockSpec(w_sp.shape, lambda i: (0, 0, 0)),
            pl.BlockSpec((tm, width), const),
            pl.BlockSpec((width, d), const),
        ],
        out_specs=pl.BlockSpec((tm, d), lambda i: (i, 0)),
        compiler_params=_params(("arbitrary",)),
        name="sgu",
    )(x, g_pre.reshape(1, d), g_post.reshape(1, d), w_uv, b_uv.reshape(1, -1), g_ln.reshape(1, -1),
      b_ln.reshape(1, -1), w_sp, bias, w_out)


def _q_scales():
    da_w, sb_w = DA_HEADS * LANES, SB_PAIRS * LANES
    one = lambda n: jnp.ones((n,), F32)
    return jnp.concatenate([one(da_w) * DA_HEAD_DIM ** -0.5, one(2 * da_w),
                            one(sb_w) * (LOG2E * SB_HEAD_DIM ** -0.5), one(2 * sb_w)])


def kernel(x, g_norm, w_ffn_gate, w_ffn_up, w_ffn_down, w_in_ab, w_out_ab, lambda_params, g_subln,
           w_uv, b_uv, g_sgu_ln, b_sgu_ln, w_spatial, b_spatial, w_out_c):
    b, s, d = x.shape
    depth = g_norm.shape[0]
    assert w_in_ab.shape[2] == 3 * LANES * (DA_HEADS + SB_PAIRS)
    assert g_subln.shape[1] == LANES and w_spatial.shape[1:] == (SGU_GROUPS, SGU_CHUNK, SGU_CHUNK)
    cast = lambda w: w.astype(BF16)
    xf = x.reshape(b * s, d)
    for l in range(depth):
        g = g_norm[l]
        xf = _ffn(xf, g[0], g[1], cast(w_ffn_gate[l, 0]), cast(w_ffn_up[l, 0]), cast(w_ffn_down[l, 0]))
        if l % 2 == 0:
            e = l // 2
            lambda_init = 0.8 - 0.6 * math.exp(-0.3 * l)
            proj = _inproj(xf, g[2], cast(w_in_ab[e] * _q_scales())).reshape(b, s, -1)
            oa = _diff_attn(proj, lambda_params[e], g_subln[e], lambda_init)
            ob = _sb_attn(proj)
            xf = _outproj(xf, oa.reshape(b * s, -1), ob.reshape(b * s, -1), cast(w_out_ab[e]), g[3])
        else:
            o = l // 2
            xf = _sgu(xf, g[2], g[3], cast(w_uv[o]), b_uv[o], g_sgu_ln[o], b_sgu_ln[o], w_spatial[o],
                      b_spatial[o], cast(w_out_c[o]))
        xf = _ffn(xf, g[4], g[5], cast(w_ffn_gate[l, 1]), cast(w_ffn_up[l, 1]), cast(w_ffn_down[l, 1]))
    return xf.reshape(b, s, d)
```

```python
import functools
import math

import jax
import jax.numpy as jnp
from jax import lax
from jax.experimental import pallas as pl
from jax.experimental.pallas import tpu as pltpu

F32 = jnp.float32
BF16 = jnp.bfloat16

EPS = 1e-6
LANES = 128
VMEM_LIMIT_BYTES = 52 * 1024 * 1024

DA_HEADS = 4
DA_HEAD_DIM = 64
SB_HEADS = 8
SB_HEAD_DIM = 64
SB_PAIRS = SB_HEADS * SB_HEAD_DIM // LANES
SGU_CHUNK = 128
SGU_GROUPS = 8
ATTN_TILE = 512
SB_BLOCK = 256
LOG2E = math.log2(math.e)
SB_DEAD = -160.0
DIFF_DEAD = 110.0
POS_RADIX = 64


def _pick(n, prefs):
    for p in prefs:
        if n % p == 0:
            return p
    return n


def _rms(x, g):
    return x * lax.rsqrt(jnp.mean(x * x, axis=-1, keepdims=True) + EPS) * g


def _gelu_tanh(x):
    c0 = math.sqrt(2.0 / math.pi)
    inner = x * (c0 + (c0 * 0.044715) * (x * x))
    return (0.5 * x) * (1.0 + jnp.tanh(inner))


def _params(sem):
    return pltpu.CompilerParams(dimension_semantics=sem, vmem_limit_bytes=VMEM_LIMIT_BYTES)


def _ffn_kernel(x_ref, gpre_ref, gpost_ref, wg_ref, wu_ref, wd_ref, o_ref, *, parts, fc):
    rows = x_ref.shape[0] // parts
    sl = [slice(p * rows, (p + 1) * rows) for p in range(parts)]
    hs = [_rms(x_ref[s, :], gpre_ref[...]).astype(BF16) for s in sl]
    accs = [None] * parts
    for c in range(wg_ref.shape[1] // fc):
        cols = slice(c * fc, (c + 1) * fc)
        for p in range(parts):
            gate = jnp.dot(hs[p], wg_ref[:, cols], preferred_element_type=F32)
            up = jnp.dot(hs[p], wu_ref[:, cols], preferred_element_type=F32)
            act = (gate * jax.nn.sigmoid(gate) * up).astype(BF16)
            down = jnp.dot(act, wd_ref[cols, :], preferred_element_type=F32)
            accs[p] = down if accs[p] is None else accs[p] + down
    for p, s in enumerate(sl):
        o_ref[s, :] = x_ref[s, :] + 0.5 * _rms(accs[p], gpost_ref[...])


def _ffn(x, g_pre, g_post, wg, wu, wd):
    n, d = x.shape
    f = wg.shape[1]
    tm = _pick(n, (1024, 512, 256, 128))
    fc = _pick(f, (256, 128))
    const = lambda i: (0, 0)
    return pl.pallas_call(
        functools.partial(_ffn_kernel, parts=2, fc=fc),
        out_shape=jax.ShapeDtypeStruct((n, d), F32),
        grid=(n // tm,),
        in_specs=[
            pl.BlockSpec((tm, d), lambda i: (i, 0)),
            pl.BlockSpec((1, d), const),
            pl.BlockSpec((1, d), const),
            pl.BlockSpec((d, f), const),
            pl.BlockSpec((d, f), const),
            pl.BlockSpec((f, d), const),
        ],
        out_specs=pl.BlockSpec((tm, d), lambda i: (i, 0)),
        compiler_params=_params(("arbitrary",)),
        name="ffn",
    )(x, g_pre.reshape(1, d), g_post.reshape(1, d), wg, wu, wd)


def _inproj_kernel(x_ref, g_ref, w_ref, o_ref, *, parts, nc):
    rows = x_ref.shape[0] // parts
    sl = [slice(p * rows, (p + 1) * rows) for p in range(parts)]
    hs = [_rms(x_ref[s, :], g_ref[...]).astype(BF16) for s in sl]
    for c in range(w_ref.shape[1] // nc):
        cols = slice(c * nc, (c + 1) * nc)
        for p, s in enumerate(sl):
            o_ref[s, cols] = jnp.dot(hs[p], w_ref[:, cols], preferred_element_type=F32).astype(o_ref.dtype)


def _inproj(x, g, w):
    n, d = x.shape
    c = w.shape[1]
    tm = _pick(n, (1024, 512, 256, 128))
    nc = _pick(c, (512, 256, 128))
    return pl.pallas_call(
        functools.partial(_inproj_kernel, parts=2, nc=nc),
        out_shape=jax.ShapeDtypeStruct((n, c), BF16),
        grid=(n // tm,),
        in_specs=[
            pl.BlockSpec((tm, d), lambda i: (i, 0)),
            pl.BlockSpec((1, d), lambda i: (0, 0)),
            pl.BlockSpec((d, c), lambda i: (0, 0)),
        ],
        out_specs=pl.BlockSpec((tm, c), lambda i: (i, 0)),
        compiler_params=_params(("arbitrary",)),
        name="inproj",
    )(x, g.reshape(1, d), w)


def _split_pair(q):
    qf = q.astype(F32)
    lane = lax.broadcasted_iota(jnp.int32, qf.shape, 1)
    lo = jnp.where(lane < DA_HEAD_DIM, qf, 0.0).astype(BF16)
    hi = jnp.where(lane >= DA_HEAD_DIM, qf, 0.0).astype(BF16)
    return lo, hi


def _qk(q, k):
    return lax.dot_general(q, k, (((1,), (1,)), ((), ())), preferred_element_type=F32)


def _diff_kernel(q_ref, k_ref, v_ref, pos_ref, lp_ref, g_ref, o_ref, m_ref, l_ref, a_ref, kmax_ref, *, t,
                 lambda_init):
    h = pl.program_id(1)
    qi = pl.program_id(2)
    lane = lax.broadcasted_iota(jnp.int32, (t, LANES), 1)
    head_v = (jnp.zeros((t, LANES), jnp.int32) + (h + 1)).astype(F32)
    slope = jnp.exp2(-8.0 / DA_HEADS * head_v)
    t0 = (jnp.zeros((t, LANES), jnp.int32) + qi * t).astype(F32)
    aug = jnp.where(lane == 0, slope * POS_RADIX,
                    jnp.where(lane == 1, slope, jnp.where(lane == 2, -slope * t0, 0.0))).astype(BF16)
    q_aug = [jnp.concatenate([qm, aug], axis=1) for qm in _split_pair(q_ref[0])]
    ones = jnp.ones((t, LANES), BF16)
    row_i = lax.broadcasted_iota(jnp.int32, (t, t), 0)
    col_i = lax.broadcasted_iota(jnp.int32, (t, t), 1)

    def trip(kjs, mask=None):
        starts = [pl.multiple_of(kj * t, t) for kj in kjs]
        ks = [jnp.concatenate([k_ref[0, pl.ds(s0, t), :], pos_ref[pl.ds(s0, t), :]], axis=1) for s0 in starts]
        vs = [jnp.concatenate([v_ref[0, pl.ds(s0, t), :], ones], axis=1) for s0 in starts]
        sss = [[_qk(q_aug[mp], k) for k in ks] for mp in range(2)]
        if mask is not None:
            sss = [[jnp.where(mask, s, -jnp.inf) for s in ss] for ss in sss]
        m_news = []
        for mp, ss in enumerate(sss):
            mloc = functools.reduce(jnp.maximum, [jnp.max(s, axis=-1, keepdims=True) for s in ss])
            m_new = jnp.broadcast_to(mloc, (t, LANES))
            m_news.append(m_new if mask is not None else jnp.maximum(m_ref[mp], m_new))
        pvs = []
        for ss, m_new in zip(sss, m_news):
            m_wide = jnp.concatenate([m_new] * (t // LANES), axis=1)
            pvs.append(sum(jnp.dot(jnp.exp(s - m_wide).astype(BF16), v, preferred_element_type=F32)
                           for s, v in zip(ss, vs)))
        for mp, (pv, m_new) in enumerate(zip(pvs, m_news)):
            if mask is None:
                alpha = jnp.exp(m_ref[mp] - m_new)
                a_ref[mp] = alpha * a_ref[mp] + pv[:, :LANES]
                l_ref[mp] = alpha * l_ref[mp] + pv[:, LANES:]
            else:
                a_ref[mp] = pv[:, :LANES]
                l_ref[mp] = pv[:, LANES:]
            m_ref[mp] = m_new

    trip([qi], col_i <= row_i)

    @pl.when(qi == 0)
    def _():
        ksq = [jnp.sum(jnp.square(k_ref[0, c * t:(c + 1) * t, :].astype(F32)), axis=-1, keepdims=True)
               for c in range(k_ref.shape[1] // t)]
        kmax_ref[0] = jnp.sqrt(jnp.max(functools.reduce(jnp.maximum, ksq)))

    qn = jnp.sqrt(jnp.sum(jnp.square(q_ref[0].astype(F32)), axis=-1, keepdims=True))
    reach = jnp.broadcast_to(qn * kmax_ref[0] + DIFF_DEAD, (t, LANES))

    def alive(kj):
        dist = (jnp.zeros((t, LANES), jnp.int32) + (qi - 1 - kj) * t).astype(F32)
        live = jnp.logical_or(reach - m_ref[0] > slope * dist, reach - m_ref[1] > slope * dist)
        return jnp.max(jnp.where(live, 1, 0)) > 0

    def body(state):
        i, _ = state
        kj = qi - 1 - 2 * i
        nxt = alive(kj - 2)
        trip([kj, kj - 1])
        return i + 1, nxt

    _, live = lax.while_loop(lambda st: jnp.logical_and(st[0] < qi // 2, st[1]), body, (0, alive(qi - 1)))

    @pl.when(jnp.logical_and(qi % 2 == 1, live))
    def _():
        trip([0])

    lp = lp_ref[...]
    lam = (jnp.exp(jnp.sum(lp[0:1] * lp[1:2], axis=-1, keepdims=True))
           - jnp.exp(jnp.sum(lp[2:3] * lp[3:4], axis=-1, keepdims=True)) + lambda_init)
    oa = a_ref[0] / l_ref[0] - lam * (a_ref[1] / l_ref[1])
    o_ref[0] = (_rms(oa, g_ref[...]) * (1.0 - lambda_init)).astype(o_ref.dtype)


def _key_positions(s):
    pos = jnp.arange(s, dtype=jnp.int32)[:, None]
    lane = jnp.arange(LANES, dtype=jnp.int32)[None, :]
    tab = jnp.where(lane == 0, pos // POS_RADIX, jnp.where(lane == 1, pos % POS_RADIX, (lane == 2).astype(jnp.int32)))
    return tab.astype(BF16)


def _diff_attn(proj, lam_p, g_subln, lambda_init):
    b, s, _ = proj.shape
    t = _pick(s, (ATTN_TILE, LANES))
    assert s // POS_RADIX <= 256 and s // t <= 256
    kern = functools.partial(_diff_kernel, t=t, lambda_init=lambda_init)
    return pl.pallas_call(
        kern,
        out_shape=jax.ShapeDtypeStruct((b, s, DA_HEADS * LANES), BF16),
        grid=(b, DA_HEADS, s // t),
        in_specs=[
            pl.BlockSpec((1, t, LANES), lambda bi, h, qi: (bi, qi, h)),
            pl.BlockSpec((1, s, LANES), lambda bi, h, qi: (bi, 0, DA_HEADS + h)),
            pl.BlockSpec((1, s, LANES), lambda bi, h, qi: (bi, 0, 2 * DA_HEADS + h)),
            pl.BlockSpec((s, LANES), lambda bi, h, qi: (0, 0)),
            pl.BlockSpec(lam_p.shape, lambda bi, h, qi: (0, 0)),
            pl.BlockSpec((1, LANES), lambda bi, h, qi: (0, 0)),
        ],
        out_specs=pl.BlockSpec((1, t, LANES), lambda bi, h, qi: (bi, qi, h)),
        scratch_shapes=[pltpu.VMEM((2, t, LANES), F32)] * 3 + [pltpu.SMEM((1,), F32)],
        compiler_params=_params(("arbitrary", "arbitrary", "arbitrary")),
        name="diff_attn",
    )(proj, proj, proj, _key_positions(s), lam_p, g_subln.reshape(1, LANES))


def _sb_kernel(q_ref, k_ref, v_ref, w_ref, o_ref, c_ref, a_ref, *, tq, tk):
    qi = pl.program_id(2)
    ratio = tq // tk
    nsub = tk // SB_BLOCK
    qs = _split_pair(q_ref[0])
    row_i = lax.broadcasted_iota(jnp.int32, (tq, SB_BLOCK), 0)
    col_i = lax.broadcasted_iota(jnp.int32, (tq, SB_BLOCK), 1)
    sign = jnp.int32(-2 ** 31)

    def heavy(kjs, diag=None):
        starts = [pl.multiple_of(kj * tk, tk) for kj in kjs]
        ks = [k_ref[0, pl.ds(s0, tk), :] for s0 in starts]
        vs = [v_ref[0, pl.ds(s0, tk), :] for s0 in starts]
        wmat = w_ref[...]
        chains = [(ti, hh) for ti in range(len(kjs)) for hh in range(2)]
        zs = [_qk(qs[hh], ks[ti]) for ti, hh in chains]
        later = [None] * len(chains)
        wts = [[None] * nsub for _ in chains]
        for sb in reversed(range(nsub)):
            zbs = [z[:, sb * SB_BLOCK:(sb + 1) * SB_BLOCK] for z in zs]
            strict = None if diag is None else col_i + (diag + sb * SB_BLOCK) < row_i
            sps = []
            for zb in zbs:
                neg_abs = lax.bitcast_convert_type(lax.bitcast_convert_type(zb, jnp.int32) | sign, F32)
                sp = jnp.maximum(zb, 0.0) + jnp.log(1.0 + jnp.exp2(neg_abs)) * LOG2E
                sps.append((sp if strict is None else jnp.where(strict, sp, 0.0)).astype(BF16))
            ces = [jnp.dot(sp, wmat, preferred_element_type=F32) for sp in sps]
            for c, (zb, ce) in enumerate(zip(zbs, ces)):
                arg = zb + ce
                if later[c] is not None:
                    arg = arg + jnp.concatenate([later[c]] * (SB_BLOCK // LANES), axis=1)
                w = jnp.exp2(arg)
                wts[c][sb] = (w if strict is None else jnp.where(strict, w, 0.0)).astype(BF16)
                total = jnp.broadcast_to(ce[:, 0:1], (tq, LANES))
                later[c] = total if later[c] is None else later[c] + total
        pvs = [jnp.dot(wts[c][0] if nsub == 1 else jnp.concatenate(wts[c], axis=1), vs[ti],
                       preferred_element_type=F32) for c, (ti, hh) in enumerate(chains)]
        return list(zip(pvs, later))

    def combine(outs):
        for c, (pv, total) in enumerate(outs):
            hh = c % 2
            carry = c_ref[hh]
            a_ref[hh] += jnp.exp2(carry) * pv
            c_ref[hh] = carry + total

    for hh, (pv, total) in enumerate(heavy([qi * ratio + ratio - 1], (ratio - 1) * tk)):
        a_ref[hh] = pv
        c_ref[hh] = total
    for i in reversed(range(ratio - 1)):
        combine(heavy([qi * ratio + i], i * tk))

    def alive():
        return jnp.max(jnp.maximum(c_ref[0], c_ref[1])) > SB_DEAD

    n = qi * ratio

    @pl.when(jnp.logical_and(n >= 1, alive()))
    def _():
        combine(heavy([n - 1]))

    m = jnp.maximum(n - 1, 0)

    def body(state):
        i, _ = state
        combine(heavy([m - 1 - 2 * i, m - 2 - 2 * i]))
        return i + 1, alive()

    _, live = lax.while_loop(lambda st: jnp.logical_and(st[0] < m // 2, st[1]), body, (0, alive()))

    @pl.when(jnp.logical_and(m % 2 == 1, live))
    def _():
        combine(heavy([0]))

    lane = lax.broadcasted_iota(jnp.int32, (tq, LANES), 1)
    o_ref[0] = jnp.where(lane < SB_HEAD_DIM, a_ref[0], a_ref[1]).astype(o_ref.dtype)


def _suffix_matrix():
    j = jnp.arange(SB_BLOCK)[:, None]
    s = jnp.arange(SB_BLOCK)[None, :]
    return -(j >= s).astype(BF16)


def _sb_attn(proj):
    b, s, _ = proj.shape
    t = _pick(s, (ATTN_TILE, SB_BLOCK))
    tk = t
    base = 3 * DA_HEADS
    kern = functools.partial(_sb_kernel, tq=t, tk=tk)
    return pl.pallas_call(
        kern,
        out_shape=jax.ShapeDtypeStruct((b, s, SB_PAIRS * LANES), BF16),
        grid=(b, SB_PAIRS, s // t),
        in_specs=[
            pl.BlockSpec((1, t, LANES), lambda bi, p, qi: (bi, qi, base + p)),
            pl.BlockSpec((1, s, LANES), lambda bi, p, qi: (bi, 0, base + SB_PAIRS + p)),
            pl.BlockSpec((1, s, LANES), lambda bi, p, qi: (bi, 0, base + 2 * SB_PAIRS + p)),
            pl.BlockSpec((SB_BLOCK, SB_BLOCK), lambda bi, p, qi: (0, 0)),
        ],
        out_specs=pl.BlockSpec((1, t, LANES), lambda bi, p, qi: (bi, qi, p)),
        scratch_shapes=[pltpu.VMEM((2, t, LANES), F32)] * 2,
        compiler_params=_params(("arbitrary", "arbitrary", "arbitrary")),
        name="sb_attn",
    )(proj, proj, proj, _suffix_matrix())


def _outproj_kernel(x_ref, oa_ref, ob_ref, wa_ref, wb_ref, g_ref, o_ref, *, parts):
    rows = x_ref.shape[0] // parts
    sl = [slice(p * rows, (p + 1) * rows) for p in range(parts)]
    ms = [jnp.dot(oa_ref[s, :], wa_ref[...], preferred_element_type=F32)
          + jnp.dot(ob_ref[s, :], wb_ref[...], preferred_element_type=F32) for s in sl]
    for s, m in zip(sl, ms):
        o_ref[s, :] = x_ref[s, :] + _rms(m, g_ref[...])


def _outproj(x, oa, ob, w, g):
    n, d = x.shape
    ca, cb = oa.shape[1], ob.shape[1]
    tm = _pick(n, (1024, 512, 256, 128))
    return pl.pallas_call(
        functools.partial(_outproj_kernel, parts=2),
        out_shape=jax.ShapeDtypeStruct((n, d), F32),
        grid=(n // tm,),
        in_specs=[
            pl.BlockSpec((tm, d), lambda i: (i, 0)),
            pl.BlockSpec((tm, ca), lambda i: (i, 0)),
            pl.BlockSpec((tm, cb), lambda i: (i, 0)),
            pl.BlockSpec((ca, d), lambda i: (0, 0)),
            pl.BlockSpec((cb, d), lambda i: (0, 0)),
            pl.BlockSpec((1, d), lambda i: (0, 0)),
        ],
        out_specs=pl.BlockSpec((tm, d), lambda i: (i, 0)),
        compiler_params=_params(("arbitrary",)),
        name="outproj",
    )(x, oa, ob, w[:ca], w[ca:], g.reshape(1, d))


def _sgu_kernel(x_ref, gpre_ref, gpost_ref, wuv_ref, buv_ref, gln_ref, bln_ref, wsp_ref, bsp_ref, wo_ref, o_ref,
                *, parts, width):
    rows = x_ref.shape[0] // parts
    sl = [slice(p * rows, (p + 1) * rows) for p in range(parts)]
    t_i = lax.broadcasted_iota(jnp.int32, (SGU_CHUNK, SGU_CHUNK), 0)
    s_i = lax.broadcasted_iota(jnp.int32, (SGU_CHUNK, SGU_CHUNK), 1)
    gch = width // SGU_GROUPS
    w_tri = [jnp.where(s_i <= t_i, wsp_ref[g], 0.0).astype(BF16) for g in range(SGU_GROUPS)]

    hs = [_rms(x_ref[s, :], gpre_ref[...]).astype(BF16) for s in sl]
    zzs = [jnp.dot(h, wuv_ref[...], preferred_element_type=F32) for h in hs]
    us, vs = [], []
    for zz in zzs:
        zz = _gelu_tanh(zz + buv_ref[...])
        v = zz[:, width:]
        vc = v - jnp.mean(v, axis=-1, keepdims=True)
        vs.append((vc * lax.rsqrt(jnp.mean(vc * vc, axis=-1, keepdims=True) + EPS) * gln_ref[...]
                   + bln_ref[...]).astype(BF16))
        us.append(zz[:, :width])
    mixeds = [jnp.concatenate(
        [jnp.concatenate([jnp.dot(w_tri[g], v[n * SGU_CHUNK:(n + 1) * SGU_CHUNK, g * gch:(g + 1) * gch],
                                  preferred_element_type=F32) for n in range(rows // SGU_CHUNK)], axis=0)
         for g in range(SGU_GROUPS)], axis=1) for v in vs]
    ys = [jnp.dot((u * (mixed + bsp_ref[...])).astype(BF16), wo_ref[...], preferred_element_type=F32)
          for u, mixed in zip(us, mixeds)]
    for s, y in zip(sl, ys):
        o_ref[s, :] = x_ref[s, :] + _rms(y, gpost_ref[...])


def _sgu(x, g_pre, g_post, w_uv, b_uv, g_ln, b_ln, w_sp, b_sp, w_out):
    n, d = x.shape
    width = w_out.shape[0]
    tm = _pick(n, (1024, 512, 256))
    parts = 2
    rows = tm // parts
    gch = width // SGU_GROUPS
    bias = jnp.tile(jnp.repeat(b_sp.T, gch, axis=1), (rows // SGU_CHUNK, 1))
    kern = functools.partial(_sgu_kernel, parts=parts, width=width)
    const = lambda i: (0, 0)
    return pl.pallas_call(
        kern,
        out_shape=jax.ShapeDtypeStruct((n, d), F32),
        grid=(n // tm,),
        in_specs=[
            pl.BlockSpec((tm, d), lambda i: (i, 0)),
            pl.BlockSpec((1, d), const),
            pl.BlockSpec((1, d), const),
            pl.BlockSpec((d, 2 * width), const),
            pl.BlockSpec((1, 2 * width), const),
            pl.BlockSpec((1, width), const),
            pl.BlockSpec((1, width), const),
            pl.BlockSpec(w_sp.shape, lambda i: (0, 0, 0)),
            pl.BlockSpec((rows, width), const),
            pl.BlockSpec((width, d), const),
        ],
        out_specs=pl.BlockSpec((tm, d), lambda i: (i, 0)),
        compiler_params=_params(("arbitrary",)),
        name="sgu",
    )(x, g_pre.reshape(1, d), g_post.reshape(1, d), w_uv, b_uv.reshape(1, -1), g_ln.reshape(1, -1),
      b_ln.reshape(1, -1), w_sp, bias, w_out)


def _q_scales():
    da_w, sb_w = DA_HEADS * LANES, SB_PAIRS * LANES
    one = lambda n: jnp.ones((n,), F32)
    return jnp.concatenate([one(da_w) * DA_HEAD_DIM ** -0.5, one(2 * da_w),
                            one(sb_w) * (LOG2E * SB_HEAD_DIM ** -0.5), one(2 * sb_w)])


def kernel(x, g_norm, w_ffn_gate, w_ffn_up, w_ffn_down, w_in_ab, w_out_ab, lambda_params, g_subln,
           w_uv, b_uv, g_sgu_ln, b_sgu_ln, w_spatial, b_spatial, w_out_c):
    b, s, d = x.shape
    depth = g_norm.shape[0]
    assert w_in_ab.shape[2] == 3 * LANES * (DA_HEADS + SB_PAIRS)
    assert g_subln.shape[1] == LANES and w_spatial.shape[1:] == (SGU_GROUPS, SGU_CHUNK, SGU_CHUNK)
    cast = lambda w: w.astype(BF16)
    xf = x.reshape(b * s, d)
    for l in range(depth):
        g = g_norm[l]
        xf = _ffn(xf, g[0], g[1], cast(w_ffn_gate[l, 0]), cast(w_ffn_up[l, 0]), cast(w_ffn_down[l, 0]))
        if l % 2 == 0:
            e = l // 2
            lambda_init = 0.8 - 0.6 * math.exp(-0.3 * l)
            proj = _inproj(xf, g[2], cast(w_in_ab[e] * _q_scales())).reshape(b, s, -1)
            oa = _diff_attn(proj, lambda_params[e], g_subln[e], lambda_init)
            ob = _sb_attn(proj)
            xf = _outproj(xf, oa.reshape(b * s, -1), ob.reshape(b * s, -1), cast(w_out_ab[e]), g[3])
        else:
            o = l // 2
            xf = _sgu(xf, g[2], g[3], cast(w_uv[o]), b_uv[o], g_sgu_ln[o], b_sgu_ln[o], w_spatial[o],
                      b_spatial[o], cast(w_out_c[o]))
        xf = _ffn(xf, g[4], g[5], cast(w_ffn_gate[l, 1]), cast(w_ffn_up[l, 1]), cast(w_ffn_down[l, 1]))
    return xf.reshape(b, s, d)
```

```python
import functools
import math

import jax
import jax.numpy as jnp
from jax import lax
from jax.experimental import pallas as pl
from jax.experimental.pallas import tpu as pltpu

F32 = jnp.float32
BF16 = jnp.bfloat16

EPS = 1e-6
LANES = 128
VMEM_LIMIT_BYTES = 52 * 1024 * 1024

DA_HEADS = 4
DA_HEAD_DIM = 64
SB_HEADS = 8
SB_HEAD_DIM = 64
SB_PAIRS = SB_HEADS * SB_HEAD_DIM // LANES
SGU_CHUNK = 128
SGU_GROUPS = 8
ATTN_TILE = 512
SB_BLOCK = 256
LOG2E = math.log2(math.e)
SB_DEAD = -160.0
DIFF_DEAD = 110.0
POS_RADIX = 64


def _pick(n, prefs):
    for p in prefs:
        if n % p == 0:
            return p
    return n


def _rms(x, g):
    return x * lax.rsqrt(jnp.mean(x * x, axis=-1, keepdims=True) + EPS) * g


def _gelu_tanh(x):
    c0 = math.sqrt(2.0 / math.pi)
    inner = x * (c0 + (c0 * 0.044715) * (x * x))
    return (0.5 * x) * (1.0 + jnp.tanh(inner))


def _params(sem):
    return pltpu.CompilerParams(dimension_semantics=sem, vmem_limit_bytes=VMEM_LIMIT_BYTES)


def _ffn_kernel(x_ref, gpre_ref, gpost_ref, wg_ref, wu_ref, wd_ref, o_ref, *, parts, fc):
    rows = x_ref.shape[0] // parts
    sl = [slice(p * rows, (p + 1) * rows) for p in range(parts)]
    hs = [_rms(x_ref[s, :], gpre_ref[...]).astype(BF16) for s in sl]
    accs = [None] * parts
    for c in range(wg_ref.shape[1] // fc):
        cols = slice(c * fc, (c + 1) * fc)
        for p in range(parts):
            gate = jnp.dot(hs[p], wg_ref[:, cols], preferred_element_type=F32)
            up = jnp.dot(hs[p], wu_ref[:, cols], preferred_element_type=F32)
            act = (gate * jax.nn.sigmoid(gate) * up).astype(BF16)
            down = jnp.dot(act, wd_ref[cols, :], preferred_element_type=F32)
            accs[p] = down if accs[p] is None else accs[p] + down
    for p, s in enumerate(sl):
        o_ref[s, :] = x_ref[s, :] + 0.5 * _rms(accs[p], gpost_ref[...])


def _ffn(x, g_pre, g_post, wg, wu, wd):
    n, d = x.shape
    f = wg.shape[1]
    tm = _pick(n, (1024, 512, 256, 128))
    fc = _pick(f, (256, 128))
    const = lambda i: (0, 0)
    return pl.pallas_call(
        functools.partial(_ffn_kernel, parts=2, fc=fc),
        out_shape=jax.ShapeDtypeStruct((n, d), F32),
        grid=(n // tm,),
        in_specs=[
            pl.BlockSpec((tm, d), lambda i: (i, 0)),
            pl.BlockSpec((1, d), const),
            pl.BlockSpec((1, d), const),
            pl.BlockSpec((d, f), const),
            pl.BlockSpec((d, f), const),
            pl.BlockSpec((f, d), const),
        ],
        out_specs=pl.BlockSpec((tm, d), lambda i: (i, 0)),
        compiler_params=_params(("arbitrary",)),
        name="ffn",
    )(x, g_pre.reshape(1, d), g_post.reshape(1, d), wg, wu, wd)


def _inproj_kernel(x_ref, g_ref, w_ref, o_ref, *, parts, nc):
    rows = x_ref.shape[0] // parts
    sl = [slice(p * rows, (p + 1) * rows) for p in range(parts)]
    hs = [_rms(x_ref[s, :], g_ref[...]).astype(BF16) for s in sl]
    for c in range(w_ref.shape[1] // nc):
        cols = slice(c * nc, (c + 1) * nc)
        for p, s in enumerate(sl):
            o_ref[s, cols] = jnp.dot(hs[p], w_ref[:, cols], preferred_element_type=F32).astype(o_ref.dtype)


def _inproj(x, g, w):
    n, d = x.shape
    c = w.shape[1]
    tm = _pick(n, (1024, 512, 256, 128))
    nc = _pick(c, (512, 256, 128))
    return pl.pallas_call(
        functools.partial(_inproj_kernel, parts=2, nc=nc),
        out_shape=jax.ShapeDtypeStruct((n, c), BF16),
        grid=(n // tm,),
        in_specs=[
            pl.BlockSpec((tm, d), lambda i: (i, 0)),
            pl.BlockSpec((1, d), lambda i: (0, 0)),
            pl.BlockSpec((d, c), lambda i: (0, 0)),
        ],
        out_specs=pl.BlockSpec((tm, c), lambda i: (i, 0)),
        compiler_params=_params(("arbitrary",)),
        name="inproj",
    )(x, g.reshape(1, d), w)


def _split_pair(q):
    qf = q.astype(F32)
    lane = lax.broadcasted_iota(jnp.int32, qf.shape, 1)
    lo = jnp.where(lane < DA_HEAD_DIM, qf, 0.0).astype(BF16)
    hi = jnp.where(lane >= DA_HEAD_DIM, qf, 0.0).astype(BF16)
    return lo, hi


def _qk(q, k):
    return lax.dot_general(q, k, (((1,), (1,)), ((), ())), preferred_element_type=F32)


def _diff_kernel(q_ref, k_ref, v_ref, pos_ref, lp_ref, g_ref, o_ref, m_ref, l_ref, a_ref, kmax_ref, *, t,
                 lambda_init):
    h = pl.program_id(1)
    qi = pl.program_id(2)
    lane = lax.broadcasted_iota(jnp.int32, (t, LANES), 1)
    head_v = (jnp.zeros((t, LANES), jnp.int32) + (h + 1)).astype(F32)
    slope = jnp.exp2(-8.0 / DA_HEADS * head_v)
    t0 = (jnp.zeros((t, LANES), jnp.int32) + qi * t).astype(F32)
    aug = jnp.where(lane == 0, slope * POS_RADIX,
                    jnp.where(lane == 1, slope, jnp.where(lane == 2, -slope * t0, 0.0))).astype(BF16)
    q_aug = [jnp.concatenate([qm, aug], axis=1) for qm in _split_pair(q_ref[0])]
    ones = jnp.ones((t, LANES), BF16)
    row_i = lax.broadcasted_iota(jnp.int32, (t, t), 0)
    col_i = lax.broadcasted_iota(jnp.int32, (t, t), 1)

    def trip(kjs, masks=None, init=False):
        starts = [pl.multiple_of(kj * t, t) for kj in kjs]
        ks = [jnp.concatenate([k_ref[0, pl.ds(s0, t), :], pos_ref[pl.ds(s0, t), :]], axis=1) for s0 in starts]
        vs = [jnp.concatenate([v_ref[0, pl.ds(s0, t), :], ones], axis=1) for s0 in starts]
        sss = [[_qk(q_aug[mp], k) for k in ks] for mp in range(2)]
        if masks is not None:
            sss = [[s if mk is None else jnp.where(mk, s, -jnp.inf) for s, mk in zip(ss, masks)] for ss in sss]
        m_news = []
        for mp, ss in enumerate(sss):
            mloc = functools.reduce(jnp.maximum, [jnp.max(s, axis=-1, keepdims=True) for s in ss])
            m_new = jnp.broadcast_to(mloc, (t, LANES))
            m_news.append(m_new if init else jnp.maximum(m_ref[mp], m_new))
        pvs = []
        for ss, m_new in zip(sss, m_news):
            m_wide = jnp.concatenate([m_new] * (t // LANES), axis=1)
            pvs.append(sum(jnp.dot(jnp.exp(s - m_wide).astype(BF16), v, preferred_element_type=F32)
                           for s, v in zip(ss, vs)))
        for mp, (pv, m_new) in enumerate(zip(pvs, m_news)):
            if init:
                a_ref[mp] = pv[:, :LANES]
                l_ref[mp] = pv[:, LANES:]
            else:
                alpha = jnp.exp(m_ref[mp] - m_new)
                a_ref[mp] = alpha * a_ref[mp] + pv[:, :LANES]
                l_ref[mp] = alpha * l_ref[mp] + pv[:, LANES:]
            m_ref[mp] = m_new

    @pl.when(qi == 0)
    def _():
        trip([qi], [col_i <= row_i], init=True)

    @pl.when(qi > 0)
    def _():
        trip([qi, qi - 1], [col_i <= row_i, None], init=True)

    n = jnp.maximum(qi - 1, 0)

    @pl.when(qi == 0)
    def _():
        ksq = [jnp.sum(jnp.square(k_ref[0, c * t:(c + 1) * t, :].astype(F32)), axis=-1, keepdims=True)
               for c in range(k_ref.shape[1] // t)]
        kmax_ref[0] = jnp.sqrt(jnp.max(functools.reduce(jnp.maximum, ksq)))

    qn = jnp.sqrt(jnp.sum(jnp.square(q_ref[0].astype(F32)), axis=-1, keepdims=True))
    reach = jnp.broadcast_to(qn * kmax_ref[0] + DIFF_DEAD, (t, LANES))

    def alive(kj):
        dist = (jnp.zeros((t, LANES), jnp.int32) + (qi - 1 - kj) * t).astype(F32)
        live = jnp.logical_or(reach - m_ref[0] > slope * dist, reach - m_ref[1] > slope * dist)
        return jnp.max(jnp.where(live, 1, 0)) > 0

    def body(state):
        i, _ = state
        kj = n - 1 - 2 * i
        nxt = alive(kj - 2)
        trip([kj, kj - 1])
        return i + 1, nxt

    _, live = lax.while_loop(lambda st: jnp.logical_and(st[0] < n // 2, st[1]), body, (0, alive(n - 1)))

    @pl.when(jnp.logical_and(n % 2 == 1, live))
    def _():
        trip([0])

    lp = lp_ref[...]
    lam = (jnp.exp(jnp.sum(lp[0:1] * lp[1:2], axis=-1, keepdims=True))
           - jnp.exp(jnp.sum(lp[2:3] * lp[3:4], axis=-1, keepdims=True)) + lambda_init)
    oa = a_ref[0] / l_ref[0] - lam * (a_ref[1] / l_ref[1])
    o_ref[0] = (_rms(oa, g_ref[...]) * (1.0 - lambda_init)).astype(o_ref.dtype)


def _key_positions(s):
    pos = jnp.arange(s, dtype=jnp.int32)[:, None]
    lane = jnp.arange(LANES, dtype=jnp.int32)[None, :]
    tab = jnp.where(lane == 0, pos // POS_RADIX, jnp.where(lane == 1, pos % POS_RADIX, (lane == 2).astype(jnp.int32)))
    return tab.astype(BF16)


def _diff_attn(proj, lam_p, g_subln, lambda_init):
    b, s, _ = proj.shape
    t = _pick(s, (ATTN_TILE, LANES))
    assert s // POS_RADIX <= 256 and s // t <= 256
    kern = functools.partial(_diff_kernel, t=t, lambda_init=lambda_init)
    return pl.pallas_call(
        kern,
        out_shape=jax.ShapeDtypeStruct((b, s, DA_HEADS * LANES), BF16),
        grid=(b, DA_HEADS, s // t),
        in_specs=[
            pl.BlockSpec((1, t, LANES), lambda bi, h, qi: (bi, qi, h)),
            pl.BlockSpec((1, s, LANES), lambda bi, h, qi: (bi, 0, DA_HEADS + h)),
            pl.BlockSpec((1, s, LANES), lambda bi, h, qi: (bi, 0, 2 * DA_HEADS + h)),
            pl.BlockSpec((s, LANES), lambda bi, h, qi: (0, 0)),
            pl.BlockSpec(lam_p.shape, lambda bi, h, qi: (0, 0)),
            pl.BlockSpec((1, LANES), lambda bi, h, qi: (0, 0)),
        ],
        out_specs=pl.BlockSpec((1, t, LANES), lambda bi, h, qi: (bi, qi, h)),
        scratch_shapes=[pltpu.VMEM((2, t, LANES), F32)] * 3 + [pltpu.SMEM((1,), F32)],
        compiler_params=_params(("arbitrary", "arbitrary", "arbitrary")),
        name="diff_attn",
    )(proj, proj, proj, _key_positions(s), lam_p, g_subln.reshape(1, LANES))


def _sb_kernel(q_ref, k_ref, v_ref, w_ref, o_ref, c_ref, a_ref, *, tq, tk):
    qi = pl.program_id(2)
    ratio = tq // tk
    nsub = tk // SB_BLOCK
    qs = _split_pair(q_ref[0])
    row_i = lax.broadcasted_iota(jnp.int32, (tq, SB_BLOCK), 0)
    col_i = lax.broadcasted_iota(jnp.int32, (tq, SB_BLOCK), 1)
    sign = jnp.int32(-2 ** 31)

    def heavy(kjs, diags=None):
        diags = diags or [None] * len(kjs)
        starts = [pl.multiple_of(kj * tk, tk) for kj in kjs]
        ks = [k_ref[0, pl.ds(s0, tk), :] for s0 in starts]
        vs = [v_ref[0, pl.ds(s0, tk), :] for s0 in starts]
        wmat = w_ref[...]
        chains = [(ti, hh) for ti in range(len(kjs)) for hh in range(2)]
        zs = [_qk(qs[hh], ks[ti]) for ti, hh in chains]
        later = [None] * len(chains)
        wts = [[None] * nsub for _ in chains]
        for sb in reversed(range(nsub)):
            zbs = [z[:, sb * SB_BLOCK:(sb + 1) * SB_BLOCK] for z in zs]
            stricts = [None if dg is None else col_i + (dg + sb * SB_BLOCK) < row_i for dg in diags]
            sps = []
            for (ti, hh), zb in zip(chains, zbs):
                neg_abs = lax.bitcast_convert_type(lax.bitcast_convert_type(zb, jnp.int32) | sign, F32)
                sp = jnp.maximum(zb, 0.0) + jnp.log(1.0 + jnp.exp2(neg_abs)) * LOG2E
                sps.append((sp if stricts[ti] is None else jnp.where(stricts[ti], sp, 0.0)).astype(BF16))
            ces = [jnp.dot(sp, wmat, preferred_element_type=F32) for sp in sps]
            for c, (zb, ce) in enumerate(zip(zbs, ces)):
                strict = stricts[chains[c][0]]
                arg = zb + ce
                if later[c] is not None:
                    arg = arg + jnp.concatenate([later[c]] * (SB_BLOCK // LANES), axis=1)
                w = jnp.exp2(arg)
                wts[c][sb] = (w if strict is None else jnp.where(strict, w, 0.0)).astype(BF16)
                total = jnp.broadcast_to(ce[:, 0:1], (tq, LANES))
                later[c] = total if later[c] is None else later[c] + total
        pvs = [jnp.dot(wts[c][0] if nsub == 1 else jnp.concatenate(wts[c], axis=1), vs[ti],
                       preferred_element_type=F32) for c, (ti, hh) in enumerate(chains)]
        return list(zip(pvs, later))

    def combine(outs, init=False):
        for c, (pv, total) in enumerate(outs):
            hh = c % 2
            if init and c < 2:
                a_ref[hh] = pv
                c_ref[hh] = total
            else:
                carry = c_ref[hh]
                a_ref[hh] += jnp.exp2(carry) * pv
                c_ref[hh] = carry + total

    n = qi * ratio
    diag_kjs = [n + i for i in reversed(range(ratio))]
    diag_offs = [i * tk for i in reversed(range(ratio))]

    @pl.when(qi == 0)
    def _():
        combine(heavy(diag_kjs, diag_offs), init=True)

    @pl.when(qi > 0)
    def _():
        combine(heavy(diag_kjs + [n - 1], diag_offs + [None]), init=True)

    def alive():
        return jnp.max(jnp.maximum(c_ref[0], c_ref[1])) > SB_DEAD

    m = jnp.maximum(n - 1, 0)

    def body(state):
        i, _ = state
        combine(heavy([m - 1 - 2 * i, m - 2 - 2 * i]))
        return i + 1, alive()

    _, live = lax.while_loop(lambda st: jnp.logical_and(st[0] < m // 2, st[1]), body, (0, alive()))

    @pl.when(jnp.logical_and(m % 2 == 1, live))
    def _():
        combine(heavy([0]))

    lane = lax.broadcasted_iota(jnp.int32, (tq, LANES), 1)
    o_ref[0] = jnp.where(lane < SB_HEAD_DIM, a_ref[0], a_ref[1]).astype(o_ref.dtype)


def _suffix_matrix():
    j = jnp.arange(SB_BLOCK)[:, None]
    s = jnp.arange(SB_BLOCK)[None, :]
    return -(j >= s).astype(BF16)


def _sb_attn(proj):
    b, s, _ = proj.shape
    t = _pick(s, (ATTN_TILE, SB_BLOCK))
    tk = SB_BLOCK
    base = 3 * DA_HEADS
    kern = functools.partial(_sb_kernel, tq=t, tk=tk)
    return pl.pallas_call(
        kern,
        out_shape=jax.ShapeDtypeStruct((b, s, SB_PAIRS * LANES), BF16),
        grid=(b, SB_PAIRS, s // t),
        in_specs=[
            pl.BlockSpec((1, t, LANES), lambda bi, p, qi: (bi, qi, base + p)),
            pl.BlockSpec((1, s, LANES), lambda bi, p, qi: (bi, 0, base + SB_PAIRS + p)),
            pl.BlockSpec((1, s, LANES), lambda bi, p, qi: (bi, 0, base + 2 * SB_PAIRS + p)),
            pl.BlockSpec((SB_BLOCK, SB_BLOCK), lambda bi, p, qi: (0, 0)),
        ],
        out_specs=pl.BlockSpec((1, t, LANES), lambda bi, p, qi: (bi, qi, p)),
        scratch_shapes=[pltpu.VMEM((2, t, LANES), F32)] * 2,
        compiler_params=_params(("arbitrary", "arbitrary", "arbitrary")),
        name="sb_attn",
    )(proj, proj, proj, _suffix_matrix())


def _outproj_kernel(x_ref, oa_ref, ob_ref, wa_ref, wb_ref, g_ref, o_ref, *, parts):
    rows = x_ref.shape[0] // parts
    sl = [slice(p * rows, (p + 1) * rows) for p in range(parts)]
    ms = [jnp.dot(oa_ref[s, :], wa_ref[...], preferred_element_type=F32)
          + jnp.dot(ob_ref[s, :], wb_ref[...], preferred_element_type=F32) for s in sl]
    for s, m in zip(sl, ms):
        o_ref[s, :] = x_ref[s, :] + _rms(m, g_ref[...])


def _outproj(x, oa, ob, w, g):
    n, d = x.shape
    ca, cb = oa.shape[1], ob.shape[1]
    tm = _pick(n, (1024, 512, 256, 128))
    return pl.pallas_call(
        functools.partial(_outproj_kernel, parts=2),
        out_shape=jax.ShapeDtypeStruct((n, d), F32),
        grid=(n // tm,),
        in_specs=[
            pl.BlockSpec((tm, d), lambda i: (i, 0)),
            pl.BlockSpec((tm, ca), lambda i: (i, 0)),
            pl.BlockSpec((tm, cb), lambda i: (i, 0)),
            pl.BlockSpec((ca, d), lambda i: (0, 0)),
            pl.BlockSpec((cb, d), lambda i: (0, 0)),
            pl.BlockSpec((1, d), lambda i: (0, 0)),
        ],
        out_specs=pl.BlockSpec((tm, d), lambda i: (i, 0)),
        compiler_params=_params(("arbitrary",)),
        name="outproj",
    )(x, oa, ob, w[:ca], w[ca:], g.reshape(1, d))


def _sgu_kernel(x_ref, gpre_ref, gpost_ref, wuv_ref, buv_ref, gln_ref, bln_ref, wsp_ref, bsp_ref, wo_ref, o_ref,
                *, parts, width):
    rows = x_ref.shape[0] // parts
    sl = [slice(p * rows, (p + 1) * rows) for p in range(parts)]
    t_i = lax.broadcasted_iota(jnp.int32, (SGU_CHUNK, SGU_CHUNK), 0)
    s_i = lax.broadcasted_iota(jnp.int32, (SGU_CHUNK, SGU_CHUNK), 1)
    gch = width // SGU_GROUPS
    w_tri = [jnp.where(s_i <= t_i, wsp_ref[g], 0.0).astype(BF16) for g in range(SGU_GROUPS)]

    hs = [_rms(x_ref[s, :], gpre_ref[...]).astype(BF16) for s in sl]
    zzs = [jnp.dot(h, wuv_ref[...], preferred_element_type=F32) for h in hs]
    us, vs = [], []
    for zz in zzs:
        zz = _gelu_tanh(zz + buv_ref[...])
        v = zz[:, width:]
        vc = v - jnp.mean(v, axis=-1, keepdims=True)
        vs.append((vc * lax.rsqrt(jnp.mean(vc * vc, axis=-1, keepdims=True) + EPS) * gln_ref[...]
                   + bln_ref[...]).astype(BF16))
        us.append(zz[:, :width])
    mixeds = [jnp.concatenate(
        [jnp.concatenate([jnp.dot(w_tri[g], v[n * SGU_CHUNK:(n + 1) * SGU_CHUNK, g * gch:(g + 1) * gch],
                                  preferred_element_type=F32) for n in range(rows // SGU_CHUNK)], axis=0)
         for g in range(SGU_GROUPS)], axis=1) for v in vs]
    ys = [jnp.dot((u * (mixed + bsp_ref[...])).astype(BF16), wo_ref[...], preferred_element_type=F32)
          for u, mixed in zip(us, mixeds)]
    for s, y in zip(sl, ys):
        o_ref[s, :] = x_ref[s, :] + _rms(y, gpost_ref[...])


def _sgu(x, g_pre, g_post, w_uv, b_uv, g_ln, b_ln, w_sp, b_sp, w_out):
    n, d = x.shape
    width = w_out.shape[0]
    tm = _pick(n, (1024, 512, 256))
    parts = 2
    rows = tm // parts
    gch = width // SGU_GROUPS
    bias = jnp.tile(jnp.repeat(b_sp.T, gch, axis=1), (rows // SGU_CHUNK, 1))
    kern = functools.partial(_sgu_kernel, parts=parts, width=width)
    const = lambda i: (0, 0)
    return pl.pallas_call(
        kern,
        out_shape=jax.ShapeDtypeStruct((n, d), F32),
        grid=(n // tm,),
        in_specs=[
            pl.BlockSpec((tm, d), lambda i: (i, 0)),
            pl.BlockSpec((1, d), const),
            pl.BlockSpec((1, d), const),
            pl.BlockSpec((d, 2 * width), const),
            pl.BlockSpec((1, 2 * width), const),
            pl.BlockSpec((1, width), const),
            pl.BlockSpec((1, width), const),
            pl.BlockSpec(w_sp.shape, lambda i: (0, 0, 0)),
            pl.BlockSpec((rows, width), const),
            pl.BlockSpec((width, d), const),
        ],
        out_specs=pl.BlockSpec((tm, d), lambda i: (i, 0)),
        compiler_params=_params(("arbitrary",)),
        name="sgu",
    )(x, g_pre.reshape(1, d), g_post.reshape(1, d), w_uv, b_uv.reshape(1, -1), g_ln.reshape(1, -1),
      b_ln.reshape(1, -1), w_sp, bias, w_out)


def _q_scales():
    da_w, sb_w = DA_HEADS * LANES, SB_PAIRS * LANES
    one = lambda n: jnp.ones((n,), F32)
    return jnp.concatenate([one(da_w) * DA_HEAD_DIM ** -0.5, one(2 * da_w),
                            one(sb_w) * (LOG2E * SB_HEAD_DIM ** -0.5), one(2 * sb_w)])


def kernel(x, g_norm, w_ffn_gate, w_ffn_up, w_ffn_down, w_in_ab, w_out_ab, lambda_params, g_subln,
           w_uv, b_uv, g_sgu_ln, b_sgu_ln, w_spatial, b_spatial, w_out_c):
    b, s, d = x.shape
    depth = g_norm.shape[0]
    assert w_in_ab.shape[2] == 3 * LANES * (DA_HEADS + SB_PAIRS)
    assert g_subln.shape[1] == LANES and w_spatial.shape[1:] == (SGU_GROUPS, SGU_CHUNK, SGU_CHUNK)
    cast = lambda w: w.astype(BF16)
    xf = x.reshape(b * s, d)
    for l in range(depth):
        g = g_norm[l]
        xf = _ffn(xf, g[0], g[1], cast(w_ffn_gate[l, 0]), cast(w_ffn_up[l, 0]), cast(w_ffn_down[l, 0]))
        if l % 2 == 0:
            e = l // 2
            lambda_init = 0.8 - 0.6 * math.exp(-0.3 * l)
            proj = _inproj(xf, g[2], cast(w_in_ab[e] * _q_scales())).reshape(b, s, -1)
            oa = _diff_attn(proj, lambda_params[e], g_subln[e], lambda_init)
            ob = _sb_attn(proj)
            xf = _outproj(xf, oa.reshape(b * s, -1), ob.reshape(b * s, -1), cast(w_out_ab[e]), g[3])
        else:
            o = l // 2
            xf = _sgu(xf, g[2], g[3], cast(w_uv[o]), b_uv[o], g_sgu_ln[o], b_sgu_ln[o], w_spatial[o],
                      b_spatial[o], cast(w_out_c[o]))
        xf = _ffn(xf, g[4], g[5], cast(w_ffn_gate[l, 1]), cast(w_ffn_up[l, 1]), cast(w_ffn_down[l, 1]))
    return xf.reshape(b, s, d)
```

```python
import functools
import math

import jax
import jax.numpy as jnp
from jax import lax
from jax.experimental import pallas as pl
from jax.experimental.pallas import tpu as pltpu

F32 = jnp.float32
BF16 = jnp.bfloat16

EPS = 1e-6
LANES = 128
VMEM_LIMIT_BYTES = 52 * 1024 * 1024

DA_HEADS = 4
DA_HEAD_DIM = 64
SB_HEADS = 8
SB_HEAD_DIM = 64
SB_PAIRS = SB_HEADS * SB_HEAD_DIM // LANES
SGU_CHUNK = 128
SGU_GROUPS = 8
ATTN_TILE = 512
SB_BLOCK = 256
LOG2E = math.log2(math.e)
SB_DEAD = -160.0
DIFF_DEAD = 110.0
POS_RADIX = 64


def _pick(n, prefs):
    for p in prefs:
        if n % p == 0:
            return p
    return n


def _rms(x, g):
    return x * lax.rsqrt(jnp.mean(x * x, axis=-1, keepdims=True) + EPS) * g


def _gelu_tanh(x):
    c0 = math.sqrt(2.0 / math.pi)
    inner = x * (c0 + (c0 * 0.044715) * (x * x))
    return (0.5 * x) * (1.0 + jnp.tanh(inner))


def _params(sem):
    return pltpu.CompilerParams(dimension_semantics=sem, vmem_limit_bytes=VMEM_LIMIT_BYTES)


def _ffn_kernel(x_ref, gpre_ref, gpost_ref, wg_ref, wu_ref, wd_ref, o_ref, *, parts, fc):
    rows = x_ref.shape[0] // parts
    sl = [slice(p * rows, (p + 1) * rows) for p in range(parts)]
    hs = [_rms(x_ref[s, :], gpre_ref[...]).astype(BF16) for s in sl]
    accs = [None] * parts
    for c in range(wg_ref.shape[1] // fc):
        cols = slice(c * fc, (c + 1) * fc)
        for p in range(parts):
            gate = jnp.dot(hs[p], wg_ref[:, cols], preferred_element_type=F32)
            up = jnp.dot(hs[p], wu_ref[:, cols], preferred_element_type=F32)
            act = (gate * jax.nn.sigmoid(gate) * up).astype(BF16)
            down = jnp.dot(act, wd_ref[cols, :], preferred_element_type=F32)
            accs[p] = down if accs[p] is None else accs[p] + down
    for p, s in enumerate(sl):
        o_ref[s, :] = x_ref[s, :] + 0.5 * _rms(accs[p], gpost_ref[...])


def _ffn(x, g_pre, g_post, wg, wu, wd, layer, half):
    n, d = x.shape
    f = wg.shape[3]
    tm = _pick(n, (1024, 512, 256, 128))
    fc = _pick(f, (256, 128))
    const = lambda i: (0, 0)
    pick = lambda i: (layer, half, 0, 0)
    return pl.pallas_call(
        functools.partial(_ffn_kernel, parts=2, fc=fc),
        out_shape=jax.ShapeDtypeStruct((n, d), F32),
        grid=(n // tm,),
        in_specs=[
            pl.BlockSpec((tm, d), lambda i: (i, 0)),
            pl.BlockSpec((1, d), const),
            pl.BlockSpec((1, d), const),
            pl.BlockSpec((None, None, d, f), pick),
            pl.BlockSpec((None, None, d, f), pick),
            pl.BlockSpec((None, None, f, d), pick),
        ],
        out_specs=pl.BlockSpec((tm, d), lambda i: (i, 0)),
        compiler_params=_params(("arbitrary",)),
        name="ffn",
    )(x, g_pre.reshape(1, d), g_post.reshape(1, d), wg, wu, wd)


def _inproj_kernel(x_ref, g_ref, w_ref, o_ref, *, parts, nc):
    rows = x_ref.shape[0] // parts
    sl = [slice(p * rows, (p + 1) * rows) for p in range(parts)]
    hs = [_rms(x_ref[s, :], g_ref[...]).astype(BF16) for s in sl]
    for c in range(w_ref.shape[1] // nc):
        cols = slice(c * nc, (c + 1) * nc)
        for p, s in enumerate(sl):
            o_ref[s, cols] = jnp.dot(hs[p], w_ref[:, cols], preferred_element_type=F32).astype(o_ref.dtype)


def _inproj(x, g, w):
    n, d = x.shape
    c = w.shape[1]
    tm = _pick(n, (1024, 512, 256, 128))
    nc = _pick(c, (512, 256, 128))
    return pl.pallas_call(
        functools.partial(_inproj_kernel, parts=2, nc=nc),
        out_shape=jax.ShapeDtypeStruct((n, c), BF16),
        grid=(n // tm,),
        in_specs=[
            pl.BlockSpec((tm, d), lambda i: (i, 0)),
            pl.BlockSpec((1, d), lambda i: (0, 0)),
            pl.BlockSpec((d, c), lambda i: (0, 0)),
        ],
        out_specs=pl.BlockSpec((tm, c), lambda i: (i, 0)),
        compiler_params=_params(("arbitrary",)),
        name="inproj",
    )(x, g.reshape(1, d), w)


def _split_pair(q):
    qf = q.astype(F32)
    lane = lax.broadcasted_iota(jnp.int32, qf.shape, 1)
    lo = jnp.where(lane < DA_HEAD_DIM, qf, 0.0).astype(BF16)
    hi = jnp.where(lane >= DA_HEAD_DIM, qf, 0.0).astype(BF16)
    return lo, hi


def _qk(q, k):
    return lax.dot_general(q, k, (((1,), (1,)), ((), ())), preferred_element_type=F32)


def _diff_kernel(q_ref, k_ref, v_ref, pos_ref, lp_ref, g_ref, o_ref, m_ref, l_ref, a_ref, kmax_ref, *, t,
                 lambda_init):
    h = pl.program_id(1)
    qi = pl.program_id(2)
    lane = lax.broadcasted_iota(jnp.int32, (t, LANES), 1)
    head_v = (jnp.zeros((t, LANES), jnp.int32) + (h + 1)).astype(F32)
    slope = jnp.exp2(-8.0 / DA_HEADS * head_v)
    t0 = (jnp.zeros((t, LANES), jnp.int32) + qi * t).astype(F32)
    aug = jnp.where(lane == 0, slope * POS_RADIX,
                    jnp.where(lane == 1, slope, jnp.where(lane == 2, -slope * t0, 0.0))).astype(BF16)
    q_aug = [jnp.concatenate([qm, aug], axis=1) for qm in _split_pair(q_ref[0])]
    ones = jnp.ones((t, LANES), BF16)
    row_i = lax.broadcasted_iota(jnp.int32, (t, t), 0)
    col_i = lax.broadcasted_iota(jnp.int32, (t, t), 1)

    def trip(kjs, masks=None, init=False):
        starts = [pl.multiple_of(kj * t, t) for kj in kjs]
        ks = [jnp.concatenate([k_ref[0, pl.ds(s0, t), :], pos_ref[pl.ds(s0, t), :]], axis=1) for s0 in starts]
        vs = [jnp.concatenate([v_ref[0, pl.ds(s0, t), :], ones], axis=1) for s0 in starts]
        sss = [[_qk(q_aug[mp], k) for k in ks] for mp in range(2)]
        if masks is not None:
            sss = [[s if mk is None else jnp.where(mk, s, -jnp.inf) for s, mk in zip(ss, masks)] for ss in sss]
        m_news = []
        for mp, ss in enumerate(sss):
            mloc = functools.reduce(jnp.maximum, [jnp.max(s, axis=-1, keepdims=True) for s in ss])
            m_new = jnp.broadcast_to(mloc, (t, LANES))
            m_news.append(m_new if init else jnp.maximum(m_ref[mp], m_new))
        pvs = []
        for ss, m_new in zip(sss, m_news):
            m_wide = jnp.concatenate([m_new] * (t // LANES), axis=1)
            pvs.append(sum(jnp.dot(jnp.exp(s - m_wide).astype(BF16), v, preferred_element_type=F32)
                           for s, v in zip(ss, vs)))
        for mp, (pv, m_new) in enumerate(zip(pvs, m_news)):
            if init:
                a_ref[mp] = pv[:, :LANES]
                l_ref[mp] = pv[:, LANES:]
            else:
                alpha = jnp.exp(m_ref[mp] - m_new)
                a_ref[mp] = alpha * a_ref[mp] + pv[:, :LANES]
                l_ref[mp] = alpha * l_ref[mp] + pv[:, LANES:]
            m_ref[mp] = m_new

    @pl.when(qi == 0)
    def _():
        trip([qi], [col_i <= row_i], init=True)

    @pl.when(qi > 0)
    def _():
        trip([qi, qi - 1], [col_i <= row_i, None], init=True)

    n = jnp.maximum(qi - 1, 0)

    @pl.when(qi == 0)
    def _():
        ksq = [jnp.sum(jnp.square(k_ref[0, c * t:(c + 1) * t, :].astype(F32)), axis=-1, keepdims=True)
               for c in range(k_ref.shape[1] // t)]
        kmax_ref[0] = jnp.sqrt(jnp.max(functools.reduce(jnp.maximum, ksq)))

    qn = jnp.sqrt(jnp.sum(jnp.square(q_ref[0].astype(F32)), axis=-1, keepdims=True))
    reach = jnp.broadcast_to(qn * kmax_ref[0] + DIFF_DEAD, (t, LANES))

    def alive(kj):
        dist = (jnp.zeros((t, LANES), jnp.int32) + (qi - 1 - kj) * t).astype(F32)
        live = jnp.logical_or(reach - m_ref[0] > slope * dist, reach - m_ref[1] > slope * dist)
        return jnp.max(jnp.where(live, 1, 0)) > 0

    def body(state):
        i, _ = state
        kj = n - 1 - 2 * i
        nxt = alive(kj - 2)
        trip([kj, kj - 1])
        return i + 1, nxt

    _, live = lax.while_loop(lambda st: jnp.logical_and(st[0] < n // 2, st[1]), body, (0, alive(n - 1)))

    @pl.when(jnp.logical_and(n % 2 == 1, live))
    def _():
        trip([0])

    lp = lp_ref[...]
    lam = (jnp.exp(jnp.sum(lp[0:1] * lp[1:2], axis=-1, keepdims=True))
           - jnp.exp(jnp.sum(lp[2:3] * lp[3:4], axis=-1, keepdims=True)) + lambda_init)
    oa = a_ref[0] / l_ref[0] - lam * (a_ref[1] / l_ref[1])
    o_ref[0] = (_rms(oa, g_ref[...]) * (1.0 - lambda_init)).astype(o_ref.dtype)


def _key_positions(s):
    pos = jnp.arange(s, dtype=jnp.int32)[:, None]
    lane = jnp.arange(LANES, dtype=jnp.int32)[None, :]
    tab = jnp.where(lane == 0, pos // POS_RADIX, jnp.where(lane == 1, pos % POS_RADIX, (lane == 2).astype(jnp.int32)))
    return tab.astype(BF16)


def _diff_attn(proj, lam_p, g_subln, lambda_init):
    b, s, _ = proj.shape
    t = _pick(s, (ATTN_TILE, LANES))
    assert s // POS_RADIX <= 256 and s // t <= 256
    kern = functools.partial(_diff_kernel, t=t, lambda_init=lambda_init)
    return pl.pallas_call(
        kern,
        out_shape=jax.ShapeDtypeStruct((b, s, DA_HEADS * LANES), BF16),
        grid=(b, DA_HEADS, s // t),
        in_specs=[
            pl.BlockSpec((1, t, LANES), lambda bi, h, qi: (bi, qi, h)),
            pl.BlockSpec((1, s, LANES), lambda bi, h, qi: (bi, 0, DA_HEADS + h)),
            pl.BlockSpec((1, s, LANES), lambda bi, h, qi: (bi, 0, 2 * DA_HEADS + h)),
            pl.BlockSpec((s, LANES), lambda bi, h, qi: (0, 0)),
            pl.BlockSpec(lam_p.shape, lambda bi, h, qi: (0, 0)),
            pl.BlockSpec((1, LANES), lambda bi, h, qi: (0, 0)),
        ],
        out_specs=pl.BlockSpec((1, t, LANES), lambda bi, h, qi: (bi, qi, h)),
        scratch_shapes=[pltpu.VMEM((2, t, LANES), F32)] * 3 + [pltpu.SMEM((1,), F32)],
        compiler_params=_params(("arbitrary", "arbitrary", "arbitrary")),
        name="diff_attn",
    )(proj, proj, proj, _key_positions(s), lam_p, g_subln.reshape(1, LANES))


def _sb_kernel(q_ref, k_ref, v_ref, w_ref, o_ref, c_ref, a_ref, *, tq, tk):
    qi = pl.program_id(2)
    ratio = tq // tk
    nsub = tk // SB_BLOCK
    qs = _split_pair(q_ref[0])
    row_i = lax.broadcasted_iota(jnp.int32, (tq, SB_BLOCK), 0)
    col_i = lax.broadcasted_iota(jnp.int32, (tq, SB_BLOCK), 1)
    sign = jnp.int32(-2 ** 31)

    def heavy(kjs, diags=None):
        diags = diags or [None] * len(kjs)
        r0s = [dg or 0 for dg in diags]
        starts = [pl.multiple_of(kj * tk, tk) for kj in kjs]
        ks = [k_ref[0, pl.ds(s0, tk), :] for s0 in starts]
        vs = [v_ref[0, pl.ds(s0, tk), :] for s0 in starts]
        wmat = w_ref[...]
        chains = [(ti, hh) for ti in range(len(kjs)) for hh in range(2)]
        zs = [_qk(qs[hh][r0s[ti]:], ks[ti]) for ti, hh in chains]
        later = [None] * len(chains)
        wts = [[None] * nsub for _ in chains]
        for sb in reversed(range(nsub)):
            zbs = [z[:, sb * SB_BLOCK:(sb + 1) * SB_BLOCK] for z in zs]
            stricts = [None if dg is None else (col_i + (dg + sb * SB_BLOCK) < row_i)[dg:] for dg in diags]
            sps = []
            for (ti, hh), zb in zip(chains, zbs):
                neg_abs = lax.bitcast_convert_type(lax.bitcast_convert_type(zb, jnp.int32) | sign, F32)
                sp = jnp.maximum(zb, 0.0) + jnp.log(1.0 + jnp.exp2(neg_abs)) * LOG2E
                sps.append((sp if stricts[ti] is None else jnp.where(stricts[ti], sp, 0.0)).astype(BF16))
            ces = [jnp.dot(sp, wmat, preferred_element_type=F32) for sp in sps]
            for c, (zb, ce) in enumerate(zip(zbs, ces)):
                strict = stricts[chains[c][0]]
                arg = zb + ce
                if later[c] is not None:
                    arg = arg + jnp.concatenate([later[c]] * (SB_BLOCK // LANES), axis=1)
                w = jnp.exp2(arg)
                wts[c][sb] = (w if strict is None else jnp.where(strict, w, 0.0)).astype(BF16)
                total = jnp.broadcast_to(ce[:, 0:1], (ce.shape[0], LANES))
                later[c] = total if later[c] is None else later[c] + total
        pvs = [jnp.dot(wts[c][0] if nsub == 1 else jnp.concatenate(wts[c], axis=1), vs[ti],
                       preferred_element_type=F32) for c, (ti, hh) in enumerate(chains)]
        return [(pv, total, r0s[ti]) for pv, total, (ti, hh) in zip(pvs, later, chains)]

    def combine(outs, init=False):
        for c, (pv, total, r0) in enumerate(outs):
            hh = c % 2
            if init and c < 2:
                if r0:
                    a_ref[hh, :r0] = jnp.zeros((r0, LANES), F32)
                    c_ref[hh, :r0] = jnp.zeros((r0, LANES), F32)
                a_ref[hh, r0:] = pv
                c_ref[hh, r0:] = total
            else:
                carry = c_ref[hh, r0:]
                a_ref[hh, r0:] += jnp.exp2(carry) * pv
                c_ref[hh, r0:] = carry + total

    n = qi * ratio
    diag_kjs = [n + i for i in reversed(range(ratio))]
    diag_offs = [i * tk for i in reversed(range(ratio))]

    @pl.when(qi == 0)
    def _():
        combine(heavy(diag_kjs, diag_offs), init=True)

    @pl.when(qi > 0)
    def _():
        combine(heavy(diag_kjs + [n - 1], diag_offs + [None]), init=True)

    def alive():
        return jnp.max(jnp.maximum(c_ref[0], c_ref[1])) > SB_DEAD

    m = jnp.maximum(n - 1, 0)

    def body(state):
        i, _ = state
        combine(heavy([m - 1 - 2 * i, m - 2 - 2 * i]))
        return i + 1, alive()

    _, live = lax.while_loop(lambda st: jnp.logical_and(st[0] < m // 2, st[1]), body, (0, alive()))

    @pl.when(jnp.logical_and(m % 2 == 1, live))
    def _():
        combine(heavy([0]))

    lane = lax.broadcasted_iota(jnp.int32, (tq, LANES), 1)
    o_ref[0] = jnp.where(lane < SB_HEAD_DIM, a_ref[0], a_ref[1]).astype(o_ref.dtype)


def _suffix_matrix():
    j = jnp.arange(SB_BLOCK)[:, None]
    s = jnp.arange(SB_BLOCK)[None, :]
    return -(j >= s).astype(BF16)


def _sb_attn(proj):
    b, s, _ = proj.shape
    t = _pick(s, (ATTN_TILE, SB_BLOCK))
    tk = SB_BLOCK
    base = 3 * DA_HEADS
    kern = functools.partial(_sb_kernel, tq=t, tk=tk)
    return pl.pallas_call(
        kern,
        out_shape=jax.ShapeDtypeStruct((b, s, SB_PAIRS * LANES), BF16),
        grid=(b, SB_PAIRS, s // t),
        in_specs=[
            pl.BlockSpec((1, t, LANES), lambda bi, p, qi: (bi, qi, base + p)),
            pl.BlockSpec((1, s, LANES), lambda bi, p, qi: (bi, 0, base + SB_PAIRS + p)),
            pl.BlockSpec((1, s, LANES), lambda bi, p, qi: (bi, 0, base + 2 * SB_PAIRS + p)),
            pl.BlockSpec((SB_BLOCK, SB_BLOCK), lambda bi, p, qi: (0, 0)),
        ],
        out_specs=pl.BlockSpec((1, t, LANES), lambda bi, p, qi: (bi, qi, p)),
        scratch_shapes=[pltpu.VMEM((2, t, LANES), F32)] * 2,
        compiler_params=_params(("arbitrary", "arbitrary", "arbitrary")),
        name="sb_attn",
    )(proj, proj, proj, _suffix_matrix())


def _outproj_kernel(x_ref, oa_ref, ob_ref, wa_ref, wb_ref, g_ref, o_ref, *, parts):
    rows = x_ref.shape[0] // parts
    sl = [slice(p * rows, (p + 1) * rows) for p in range(parts)]
    ms = [jnp.dot(oa_ref[s, :], wa_ref[...], preferred_element_type=F32)
          + jnp.dot(ob_ref[s, :], wb_ref[...], preferred_element_type=F32) for s in sl]
    for s, m in zip(sl, ms):
        o_ref[s, :] = x_ref[s, :] + _rms(m, g_ref[...])


def _outproj(x, oa, ob, w, g):
    n, d = x.shape
    ca, cb = oa.shape[1], ob.shape[1]
    tm = _pick(n, (1024, 512, 256, 128))
    return pl.pallas_call(
        functools.partial(_outproj_kernel, parts=2),
        out_shape=jax.ShapeDtypeStruct((n, d), F32),
        grid=(n // tm,),
        in_specs=[
            pl.BlockSpec((tm, d), lambda i: (i, 0)),
            pl.BlockSpec((tm, ca), lambda i: (i, 0)),
            pl.BlockSpec((tm, cb), lambda i: (i, 0)),
            pl.BlockSpec((ca, d), lambda i: (0, 0)),
            pl.BlockSpec((cb, d), lambda i: (0, 0)),
            pl.BlockSpec((1, d), lambda i: (0, 0)),
        ],
        out_specs=pl.BlockSpec((tm, d), lambda i: (i, 0)),
        compiler_params=_params(("arbitrary",)),
        name="outproj",
    )(x, oa, ob, w[:ca], w[ca:], g.reshape(1, d))


def _sgu_kernel(x_ref, gpre_ref, gpost_ref, wuv_ref, buv_ref, gln_ref, bln_ref, wsp_ref, bsp_ref, wo_ref, o_ref,
                *, parts, width):
    rows = x_ref.shape[0] // parts
    sl = [slice(p * rows, (p + 1) * rows) for p in range(parts)]
    t_i = lax.broadcasted_iota(jnp.int32, (SGU_CHUNK, SGU_CHUNK), 0)
    s_i = lax.broadcasted_iota(jnp.int32, (SGU_CHUNK, SGU_CHUNK), 1)
    gch = width // SGU_GROUPS
    w_tri = [jnp.where(s_i <= t_i, wsp_ref[g], 0.0).astype(BF16) for g in range(SGU_GROUPS)]

    hs = [_rms(x_ref[s, :], gpre_ref[...]).astype(BF16) for s in sl]
    zzs = [jnp.dot(h, wuv_ref[...], preferred_element_type=F32) for h in hs]
    us, vs = [], []
    for zz in zzs:
        zz = _gelu_tanh(zz + buv_ref[...])
        v = zz[:, width:]
        vc = v - jnp.mean(v, axis=-1, keepdims=True)
        vs.append((vc * lax.rsqrt(jnp.mean(vc * vc, axis=-1, keepdims=True) + EPS) * gln_ref[...]
                   + bln_ref[...]).astype(BF16))
        us.append(zz[:, :width])
    mixeds = [jnp.concatenate(
        [jnp.concatenate([jnp.dot(w_tri[g], v[n * SGU_CHUNK:(n + 1) * SGU_CHUNK, g * gch:(g + 1) * gch],
                                  preferred_element_type=F32) for n in range(rows // SGU_CHUNK)], axis=0)
         for g in range(SGU_GROUPS)], axis=1) for v in vs]
    ys = [jnp.dot((u * (mixed + bsp_ref[...])).astype(BF16), wo_ref[...], preferred_element_type=F32)
          for u, mixed in zip(us, mixeds)]
    for s, y in zip(sl, ys):
        o_ref[s, :] = x_ref[s, :] + _rms(y, gpost_ref[...])


def _sgu(x, g_pre, g_post, w_uv, b_uv, g_ln, b_ln, w_sp, b_sp, w_out):
    n, d = x.shape
    width = w_out.shape[0]
    tm = _pick(n, (1024, 512, 256))
    parts = 2
    rows = tm // parts
    gch = width // SGU_GROUPS
    bias = jnp.tile(jnp.repeat(b_sp.T, gch, axis=1), (rows // SGU_CHUNK, 1))
    kern = functools.partial(_sgu_kernel, parts=parts, width=width)
    const = lambda i: (0, 0)
    return pl.pallas_call(
        kern,
        out_shape=jax.ShapeDtypeStruct((n, d), F32),
        grid=(n // tm,),
        in_specs=[
            pl.BlockSpec((tm, d), lambda i: (i, 0)),
            pl.BlockSpec((1, d), const),
            pl.BlockSpec((1, d), const),
            pl.BlockSpec((d, 2 * width), const),
            pl.BlockSpec((1, 2 * width), const),
            pl.BlockSpec((1, width), const),
            pl.BlockSpec((1, width), const),
            pl.BlockSpec(w_sp.shape, lambda i: (0, 0, 0)),
            pl.BlockSpec((rows, width), const),
            pl.BlockSpec((width, d), const),
        ],
        out_specs=pl.BlockSpec((tm, d), lambda i: (i, 0)),
        compiler_params=_params(("arbitrary",)),
        name="sgu",
    )(x, g_pre.reshape(1, d), g_post.reshape(1, d), w_uv, b_uv.reshape(1, -1), g_ln.reshape(1, -1),
      b_ln.reshape(1, -1), w_sp, bias, w_out)


def _q_scales():
    da_w, sb_w = DA_HEADS * LANES, SB_PAIRS * LANES
    one = lambda n: jnp.ones((n,), F32)
    return jnp.concatenate([one(da_w) * DA_HEAD_DIM ** -0.5, one(2 * da_w),
                            one(sb_w) * (LOG2E * SB_HEAD_DIM ** -0.5), one(2 * sb_w)])


def kernel(x, g_norm, w_ffn_gate, w_ffn_up, w_ffn_down, w_in_ab, w_out_ab, lambda_params, g_subln,
           w_uv, b_uv, g_sgu_ln, b_sgu_ln, w_spatial, b_spatial, w_out_c):
    b, s, d = x.shape
    depth = g_norm.shape[0]
    assert w_in_ab.shape[2] == 3 * LANES * (DA_HEADS + SB_PAIRS)
    assert g_subln.shape[1] == LANES and w_spatial.shape[1:] == (SGU_GROUPS, SGU_CHUNK, SGU_CHUNK)
    cast = lambda w: w.astype(BF16)
    wg_all, wu_all, wd_all = cast(w_ffn_gate), cast(w_ffn_up), cast(w_ffn_down)
    xf = x.reshape(b * s, d)
    for l in range(depth):
        g = g_norm[l]
        xf = _ffn(xf, g[0], g[1], wg_all, wu_all, wd_all, l, 0)
        if l % 2 == 0:
            e = l // 2
            lambda_init = 0.8 - 0.6 * math.exp(-0.3 * l)
            proj = _inproj(xf, g[2], cast(w_in_ab[e] * _q_scales())).reshape(b, s, -1)
            oa = _diff_attn(proj, lambda_params[e], g_subln[e], lambda_init)
            ob = _sb_attn(proj)
            xf = _outproj(xf, oa.reshape(b * s, -1), ob.reshape(b * s, -1), cast(w_out_ab[e]), g[3])
        else:
            o = l // 2
            xf = _sgu(xf, g[2], g[3], cast(w_uv[o]), b_uv[o], g_sgu_ln[o], b_sgu_ln[o], w_spatial[o],
                      b_spatial[o], cast(w_out_c[o]))
        xf = _ffn(xf, g[4], g[5], wg_all, wu_all, wd_all, l, 1)
    return xf.reshape(b, s, d)
```

```python
import functools
import math

import jax
import jax.numpy as jnp
from jax import lax
from jax.experimental import pallas as pl
from jax.experimental.pallas import tpu as pltpu

F32 = jnp.float32
BF16 = jnp.bfloat16

EPS = 1e-6
LANES = 128
VMEM_LIMIT_BYTES = 52 * 1024 * 1024

DA_HEADS = 4
DA_HEAD_DIM = 64
SB_HEADS = 8
SB_HEAD_DIM = 64
SB_PAIRS = SB_HEADS * SB_HEAD_DIM // LANES
SGU_CHUNK = 128
SGU_GROUPS = 8
ATTN_TILE = 512
SB_BLOCK = 256
LOG2E = math.log2(math.e)
SB_DEAD = -160.0
DIFF_DEAD = 110.0
POS_RADIX = 64


def _pick(n, prefs):
    for p in prefs:
        if n % p == 0:
            return p
    return n


def _rms(x, g):
    return x * lax.rsqrt(jnp.mean(x * x, axis=-1, keepdims=True) + EPS) * g


def _gelu_tanh(x):
    c0 = math.sqrt(2.0 / math.pi)
    inner = x * (c0 + (c0 * 0.044715) * (x * x))
    return (0.5 * x) * (1.0 + jnp.tanh(inner))


def _params(sem):
    return pltpu.CompilerParams(dimension_semantics=sem, vmem_limit_bytes=VMEM_LIMIT_BYTES)


def _ffn_kernel(x_ref, gpre_ref, gpost_ref, wg_ref, wu_ref, wd_ref, o_ref, *, parts, fc):
    rows = x_ref.shape[0] // parts
    sl = [slice(p * rows, (p + 1) * rows) for p in range(parts)]
    hs = [_rms(x_ref[s, :], gpre_ref[...]).astype(BF16) for s in sl]
    accs = [None] * parts
    for c in range(wg_ref.shape[1] // fc):
        cols = slice(c * fc, (c + 1) * fc)
        for p in range(parts):
            gate = jnp.dot(hs[p], wg_ref[:, cols], preferred_element_type=F32)
            up = jnp.dot(hs[p], wu_ref[:, cols], preferred_element_type=F32)
            act = (gate * jax.nn.sigmoid(gate) * up).astype(BF16)
            down = jnp.dot(act, wd_ref[cols, :], preferred_element_type=F32)
            accs[p] = down if accs[p] is None else accs[p] + down
    for p, s in enumerate(sl):
        o_ref[s, :] = x_ref[s, :] + 0.5 * _rms(accs[p], gpost_ref[...])


def _ffn(x, g_pre, g_post, wg, wu, wd, layer, half):
    n, d = x.shape
    f = wg.shape[3]
    tm = _pick(n, (1024, 512, 256, 128))
    fc = _pick(f, (256, 128))
    const = lambda i: (0, 0)
    pick = lambda i: (layer, half, 0, 0)
    return pl.pallas_call(
        functools.partial(_ffn_kernel, parts=2, fc=fc),
        out_shape=jax.ShapeDtypeStruct((n, d), F32),
        grid=(n // tm,),
        in_specs=[
            pl.BlockSpec((tm, d), lambda i: (i, 0)),
            pl.BlockSpec((1, d), const),
            pl.BlockSpec((1, d), const),
            pl.BlockSpec((None, None, d, f), pick),
            pl.BlockSpec((None, None, d, f), pick),
            pl.BlockSpec((None, None, f, d), pick),
        ],
        out_specs=pl.BlockSpec((tm, d), lambda i: (i, 0)),
        compiler_params=_params(("arbitrary",)),
        name="ffn",
    )(x, g_pre.reshape(1, d), g_post.reshape(1, d), wg, wu, wd)


def _inproj_kernel(x_ref, g_ref, w_ref, o_ref, *, parts, nc):
    rows = x_ref.shape[0] // parts
    sl = [slice(p * rows, (p + 1) * rows) for p in range(parts)]
    hs = [_rms(x_ref[s, :], g_ref[...]).astype(BF16) for s in sl]
    for c in range(w_ref.shape[1] // nc):
        cols = slice(c * nc, (c + 1) * nc)
        for p, s in enumerate(sl):
            o_ref[s, cols] = jnp.dot(hs[p], w_ref[:, cols], preferred_element_type=F32).astype(o_ref.dtype)


def _inproj(x, g, w):
    n, d = x.shape
    c = w.shape[1]
    tm = _pick(n, (1024, 512, 256, 128))
    nc = _pick(c, (512, 256, 128))
    return pl.pallas_call(
        functools.partial(_inproj_kernel, parts=2, nc=nc),
        out_shape=jax.ShapeDtypeStruct((n, c), BF16),
        grid=(n // tm,),
        in_specs=[
            pl.BlockSpec((tm, d), lambda i: (i, 0)),
            pl.BlockSpec((1, d), lambda i: (0, 0)),
            pl.BlockSpec((d, c), lambda i: (0, 0)),
        ],
        out_specs=pl.BlockSpec((tm, c), lambda i: (i, 0)),
        compiler_params=_params(("arbitrary",)),
        name="inproj",
    )(x, g.reshape(1, d), w)


def _split_pair(q):
    qf = q.astype(F32)
    lane = lax.broadcasted_iota(jnp.int32, qf.shape, 1)
    lo = jnp.where(lane < DA_HEAD_DIM, qf, 0.0).astype(BF16)
    hi = jnp.where(lane >= DA_HEAD_DIM, qf, 0.0).astype(BF16)
    return lo, hi


def _qk(q, k):
    return lax.dot_general(q, k, (((1,), (1,)), ((), ())), preferred_element_type=F32)


def _diff_kernel(q_ref, k_ref, v_ref, pos_ref, lp_ref, g_ref, o_ref, m_ref, l_ref, a_ref, kmax_ref, live_ref, *, t,
                 lambda_init):
    h = pl.program_id(1)
    qi = pl.program_id(2)
    lane = lax.broadcasted_iota(jnp.int32, (t, LANES), 1)
    head_v = (jnp.zeros((t, LANES), jnp.int32) + (h + 1)).astype(F32)
    slope = jnp.exp2(-8.0 / DA_HEADS * head_v)
    t0 = (jnp.zeros((t, LANES), jnp.int32) + qi * t).astype(F32)
    aug = jnp.where(lane == 0, slope * POS_RADIX,
                    jnp.where(lane == 1, slope, jnp.where(lane == 2, -slope * t0, 0.0))).astype(BF16)
    q_aug = [jnp.concatenate([qm, aug], axis=1) for qm in _split_pair(q_ref[0])]
    ones = jnp.ones((t, LANES), BF16)
    row_i = lax.broadcasted_iota(jnp.int32, (t, t), 0)
    col_i = lax.broadcasted_iota(jnp.int32, (t, t), 1)

    def trip(kjs, masks=None, init=False, next_kj=None):
        starts = [pl.multiple_of(kj * t, t) for kj in kjs]
        ks = [jnp.concatenate([k_ref[0, pl.ds(s0, t), :], pos_ref[pl.ds(s0, t), :]], axis=1) for s0 in starts]
        vs = [jnp.concatenate([v_ref[0, pl.ds(s0, t), :], ones], axis=1) for s0 in starts]
        sss = [[_qk(q_aug[mp], k) for k in ks] for mp in range(2)]
        if masks is not None:
            sss = [[s if mk is None else jnp.where(mk, s, -jnp.inf) for s, mk in zip(ss, masks)] for ss in sss]
        m_news = []
        for mp, ss in enumerate(sss):
            mloc = functools.reduce(jnp.maximum, [jnp.max(s, axis=-1, keepdims=True) for s in ss])
            m_new = jnp.broadcast_to(mloc, (t, LANES))
            m_news.append(m_new if init else jnp.maximum(m_ref[mp], m_new))
        pvs = []
        for ss, m_new in zip(sss, m_news):
            m_wide = jnp.concatenate([m_new] * (t // LANES), axis=1)
            pvs.append(sum(jnp.dot(jnp.exp(s - m_wide).astype(BF16), v, preferred_element_type=F32)
                           for s, v in zip(ss, vs)))
        for mp, (pv, m_new) in enumerate(zip(pvs, m_news)):
            if init:
                a_ref[mp] = pv[:, :LANES]
                l_ref[mp] = pv[:, LANES:]
            else:
                alpha = jnp.exp(m_ref[mp] - m_new)
                a_ref[mp] = alpha * a_ref[mp] + pv[:, :LANES]
                l_ref[mp] = alpha * l_ref[mp] + pv[:, LANES:]
            m_ref[mp] = m_new
        if next_kj is not None:
            dist = (jnp.zeros((t, LANES), jnp.int32) + (qi - 1 - next_kj) * t).astype(F32)
            live = jnp.logical_or(reach - m_news[0] > slope * dist, reach - m_news[1] > slope * dist)
            live_ref[0] = jnp.max(jnp.where(live, 1, 0))

    @pl.when(qi == 0)
    def _():
        ksq = [jnp.sum(jnp.square(k_ref[0, c * t:(c + 1) * t, :].astype(F32)), axis=-1, keepdims=True)
               for c in range(k_ref.shape[1] // t)]
        kmax_ref[0] = jnp.sqrt(jnp.max(functools.reduce(jnp.maximum, ksq)))
        live_ref[0] = 0

    qn = jnp.sqrt(jnp.sum(jnp.square(q_ref[0].astype(F32)), axis=-1, keepdims=True))
    reach = jnp.broadcast_to(qn * kmax_ref[0] + DIFF_DEAD, (t, LANES))
    n = jnp.maximum(qi - 1, 0)

    @pl.when(qi == 0)
    def _():
        trip([qi], [col_i <= row_i], init=True)

    @pl.when(qi > 0)
    def _():
        trip([qi, qi - 1], [col_i <= row_i, None], init=True, next_kj=n - 1)

    def body(state):
        i, _ = state
        kj = n - 1 - 2 * i
        trip([kj, kj - 1], next_kj=kj - 2)
        return i + 1, live_ref[0] > 0

    _, live = lax.while_loop(lambda st: jnp.logical_and(st[0] < n // 2, st[1]), body, (0, live_ref[0] > 0))

    @pl.when(jnp.logical_and(n % 2 == 1, live))
    def _():
        trip([0])

    lp = lp_ref[...]
    lam = (jnp.exp(jnp.sum(lp[0:1] * lp[1:2], axis=-1, keepdims=True))
           - jnp.exp(jnp.sum(lp[2:3] * lp[3:4], axis=-1, keepdims=True)) + lambda_init)
    oa = a_ref[0] / l_ref[0] - lam * (a_ref[1] / l_ref[1])
    o_ref[0] = (_rms(oa, g_ref[...]) * (1.0 - lambda_init)).astype(o_ref.dtype)


def _key_positions(s):
    pos = jnp.arange(s, dtype=jnp.int32)[:, None]
    lane = jnp.arange(LANES, dtype=jnp.int32)[None, :]
    tab = jnp.where(lane == 0, pos // POS_RADIX, jnp.where(lane == 1, pos % POS_RADIX, (lane == 2).astype(jnp.int32)))
    return tab.astype(BF16)


def _diff_attn(proj, lam_p, g_subln, lambda_init):
    b, s, _ = proj.shape
    t = _pick(s, (ATTN_TILE, LANES))
    assert s // POS_RADIX <= 256 and s // t <= 256
    kern = functools.partial(_diff_kernel, t=t, lambda_init=lambda_init)
    return pl.pallas_call(
        kern,
        out_shape=jax.ShapeDtypeStruct((b, s, DA_HEADS * LANES), BF16),
        grid=(b, DA_HEADS, s // t),
        in_specs=[
            pl.BlockSpec((1, t, LANES), lambda bi, h, qi: (bi, qi, h)),
            pl.BlockSpec((1, s, LANES), lambda bi, h, qi: (bi, 0, DA_HEADS + h)),
            pl.BlockSpec((1, s, LANES), lambda bi, h, qi: (bi, 0, 2 * DA_HEADS + h)),
            pl.BlockSpec((s, LANES), lambda bi, h, qi: (0, 0)),
            pl.BlockSpec(lam_p.shape, lambda bi, h, qi: (0, 0)),
            pl.BlockSpec((1, LANES), lambda bi, h, qi: (0, 0)),
        ],
        out_specs=pl.BlockSpec((1, t, LANES), lambda bi, h, qi: (bi, qi, h)),
        scratch_shapes=[pltpu.VMEM((2, t, LANES), F32)] * 3 + [pltpu.SMEM((1,), F32), pltpu.SMEM((1,), jnp.int32)],
        compiler_params=_params(("arbitrary", "arbitrary", "arbitrary")),
        name="diff_attn",
    )(proj, proj, proj, _key_positions(s), lam_p, g_subln.reshape(1, LANES))


def _sb_kernel(q_ref, k_ref, v_ref, w_ref, o_ref, c_ref, a_ref, live_ref, *, tq, tk):
    qi = pl.program_id(2)
    ratio = tq // tk
    nsub = tk // SB_BLOCK
    qs = _split_pair(q_ref[0])
    row_i = lax.broadcasted_iota(jnp.int32, (tq, SB_BLOCK), 0)
    col_i = lax.broadcasted_iota(jnp.int32, (tq, SB_BLOCK), 1)
    sign = jnp.int32(-2 ** 31)

    def heavy(kjs, diags=None):
        diags = diags or [None] * len(kjs)
        r0s = [dg or 0 for dg in diags]
        starts = [pl.multiple_of(kj * tk, tk) for kj in kjs]
        ks = [k_ref[0, pl.ds(s0, tk), :] for s0 in starts]
        vs = [v_ref[0, pl.ds(s0, tk), :] for s0 in starts]
        wmat = w_ref[...]
        chains = [(ti, hh) for ti in range(len(kjs)) for hh in range(2)]
        zs = [_qk(qs[hh][r0s[ti]:], ks[ti]) for ti, hh in chains]
        later = [None] * len(chains)
        wts = [[None] * nsub for _ in chains]
        for sb in reversed(range(nsub)):
            zbs = [z[:, sb * SB_BLOCK:(sb + 1) * SB_BLOCK] for z in zs]
            stricts = [None if dg is None else (col_i + (dg + sb * SB_BLOCK) < row_i)[dg:] for dg in diags]
            sps = []
            for (ti, hh), zb in zip(chains, zbs):
                neg_abs = lax.bitcast_convert_type(lax.bitcast_convert_type(zb, jnp.int32) | sign, F32)
                sp = jnp.maximum(zb, 0.0) + jnp.log(1.0 + jnp.exp2(neg_abs)) * LOG2E
                sps.append((sp if stricts[ti] is None else jnp.where(stricts[ti], sp, 0.0)).astype(BF16))
            ces = [jnp.dot(sp, wmat, preferred_element_type=F32) for sp in sps]
            for c, (zb, ce) in enumerate(zip(zbs, ces)):
                strict = stricts[chains[c][0]]
                arg = zb + ce
                if later[c] is not None:
                    arg = arg + jnp.concatenate([later[c]] * (SB_BLOCK // LANES), axis=1)
                w = jnp.exp2(arg)
                wts[c][sb] = (w if strict is None else jnp.where(strict, w, 0.0)).astype(BF16)
                total = jnp.broadcast_to(ce[:, 0:1], (ce.shape[0], LANES))
                later[c] = total if later[c] is None else later[c] + total
        pvs = [jnp.dot(wts[c][0] if nsub == 1 else jnp.concatenate(wts[c], axis=1), vs[ti],
                       preferred_element_type=F32) for c, (ti, hh) in enumerate(chains)]
        return [(pv, total, r0s[ti]) for pv, total, (ti, hh) in zip(pvs, later, chains)]

    def combine(outs, init=False):
        carry = [None, None] if init else [c_ref[0], c_ref[1]]
        before = []
        for c, (pv, total, r0) in enumerate(outs):
            hh = c % 2
            before.append(carry[hh])
            if r0:
                total = jnp.concatenate([jnp.zeros((r0, LANES), F32), total], axis=0)
            carry[hh] = total if carry[hh] is None else carry[hh] + total
        c_ref[0], c_ref[1] = carry
        live_ref[0] = jnp.max(jnp.where(jnp.maximum(carry[0], carry[1]) > SB_DEAD, 1, 0))
        for c, (pv, total, r0) in enumerate(outs):
            hh = c % 2
            if before[c] is None:
                if r0:
                    a_ref[hh, :r0] = jnp.zeros((r0, LANES), F32)
                a_ref[hh, r0:] = pv
            else:
                a_ref[hh, r0:] += jnp.exp2(before[c][r0:]) * pv

    n = qi * ratio
    diag_kjs = [n + i for i in reversed(range(ratio))]
    diag_offs = [i * tk for i in reversed(range(ratio))]

    @pl.when(qi == 0)
    def _():
        combine(heavy(diag_kjs, diag_offs), init=True)

    @pl.when(qi > 0)
    def _():
        combine(heavy(diag_kjs + [n - 1], diag_offs + [None]), init=True)

    m = jnp.maximum(n - 1, 0)

    def body(state):
        i, _ = state
        combine(heavy([m - 1 - 2 * i, m - 2 - 2 * i]))
        return i + 1, live_ref[0] > 0

    _, live = lax.while_loop(lambda st: jnp.logical_and(st[0] < m // 2, st[1]), body, (0, live_ref[0] > 0))

    @pl.when(jnp.logical_and(m % 2 == 1, live))
    def _():
        combine(heavy([0]))

    lane = lax.broadcasted_iota(jnp.int32, (tq, LANES), 1)
    o_ref[0] = jnp.where(lane < SB_HEAD_DIM, a_ref[0], a_ref[1]).astype(o_ref.dtype)


def _suffix_matrix():
    j = jnp.arange(SB_BLOCK)[:, None]
    s = jnp.arange(SB_BLOCK)[None, :]
    return -(j >= s).astype(BF16)


def _sb_attn(proj):
    b, s, _ = proj.shape
    t = _pick(s, (ATTN_TILE, SB_BLOCK))
    tk = SB_BLOCK
    base = 3 * DA_HEADS
    kern = functools.partial(_sb_kernel, tq=t, tk=tk)
    return pl.pallas_call(
        kern,
        out_shape=jax.ShapeDtypeStruct((b, s, SB_PAIRS * LANES), BF16),
        grid=(b, SB_PAIRS, s // t),
        in_specs=[
            pl.BlockSpec((1, t, LANES), lambda bi, p, qi: (bi, qi, base + p)),
            pl.BlockSpec((1, s, LANES), lambda bi, p, qi: (bi, 0, base + SB_PAIRS + p)),
            pl.BlockSpec((1, s, LANES), lambda bi, p, qi: (bi, 0, base + 2 * SB_PAIRS + p)),
            pl.BlockSpec((SB_BLOCK, SB_BLOCK), lambda bi, p, qi: (0, 0)),
        ],
        out_specs=pl.BlockSpec((1, t, LANES), lambda bi, p, qi: (bi, qi, p)),
        scratch_shapes=[pltpu.VMEM((2, t, LANES), F32)] * 2 + [pltpu.SMEM((1,), jnp.int32)],
        compiler_params=_params(("arbitrary", "arbitrary", "arbitrary")),
        name="sb_attn",
    )(proj, proj, proj, _suffix_matrix())


def _outproj_kernel(x_ref, oa_ref, ob_ref, wa_ref, wb_ref, g_ref, o_ref, *, parts):
    rows = x_ref.shape[0] // parts
    sl = [slice(p * rows, (p + 1) * rows) for p in range(parts)]
    ms = [jnp.dot(oa_ref[s, :], wa_ref[...], preferred_element_type=F32)
          + jnp.dot(ob_ref[s, :], wb_ref[...], preferred_element_type=F32) for s in sl]
    for s, m in zip(sl, ms):
        o_ref[s, :] = x_ref[s, :] + _rms(m, g_ref[...])


def _outproj(x, oa, ob, w, g):
    n, d = x.shape
    ca, cb = oa.shape[1], ob.shape[1]
    tm = _pick(n, (1024, 512, 256, 128))
    return pl.pallas_call(
        functools.partial(_outproj_kernel, parts=2),
        out_shape=jax.ShapeDtypeStruct((n, d), F32),
        grid=(n // tm,),
        in_specs=[
            pl.BlockSpec((tm, d), lambda i: (i, 0)),
            pl.BlockSpec((tm, ca), lambda i: (i, 0)),
            pl.BlockSpec((tm, cb), lambda i: (i, 0)),
            pl.BlockSpec((ca, d), lambda i: (0, 0)),
            pl.BlockSpec((cb, d), lambda i: (0, 0)),
            pl.BlockSpec((1, d), lambda i: (0, 0)),
        ],
        out_specs=pl.BlockSpec((tm, d), lambda i: (i, 0)),
        compiler_params=_params(("arbitrary",)),
        name="outproj",
    )(x, oa, ob, w[:ca], w[ca:], g.reshape(1, d))


def _sgu_kernel(x_ref, gpre_ref, gpost_ref, wuv_ref, buv_ref, gln_ref, bln_ref, wsp_ref, bsp_ref, wo_ref, o_ref,
                *, parts, width):
    rows = x_ref.shape[0] // parts
    sl = [slice(p * rows, (p + 1) * rows) for p in range(parts)]
    t_i = lax.broadcasted_iota(jnp.int32, (SGU_CHUNK, SGU_CHUNK), 0)
    s_i = lax.broadcasted_iota(jnp.int32, (SGU_CHUNK, SGU_CHUNK), 1)
    gch = width // SGU_GROUPS
    w_tri = [jnp.where(s_i <= t_i, wsp_ref[g], 0.0).astype(BF16) for g in range(SGU_GROUPS)]

    hs = [_rms(x_ref[s, :], gpre_ref[...]).astype(BF16) for s in sl]
    zzs = [jnp.dot(h, wuv_ref[...], preferred_element_type=F32) for h in hs]
    us, vs = [], []
    for zz in zzs:
        zz = _gelu_tanh(zz + buv_ref[...])
        v = zz[:, width:]
        vc = v - jnp.mean(v, axis=-1, keepdims=True)
        vs.append((vc * lax.rsqrt(jnp.mean(vc * vc, axis=-1, keepdims=True) + EPS) * gln_ref[...]
                   + bln_ref[...]).astype(BF16))
        us.append(zz[:, :width])
    mixeds = [jnp.concatenate(
        [jnp.concatenate([jnp.dot(w_tri[g], v[n * SGU_CHUNK:(n + 1) * SGU_CHUNK, g * gch:(g + 1) * gch],
                                  preferred_element_type=F32) for n in range(rows // SGU_CHUNK)], axis=0)
         for g in range(SGU_GROUPS)], axis=1) for v in vs]
    ys = [jnp.dot((u * (mixed + bsp_ref[...])).astype(BF16), wo_ref[...], preferred_element_type=F32)
          for u, mixed in zip(us, mixeds)]
    for s, y in zip(sl, ys):
        o_ref[s, :] = x_ref[s, :] + _rms(y, gpost_ref[...])


def _sgu(x, g_pre, g_post, w_uv, b_uv, g_ln, b_ln, w_sp, b_sp, w_out):
    n, d = x.shape
    width = w_out.shape[0]
    tm = _pick(n, (1024, 512, 256))
    parts = 2
    rows = tm // parts
    gch = width // SGU_GROUPS
    bias = jnp.tile(jnp.repeat(b_sp.T, gch, axis=1), (rows // SGU_CHUNK, 1))
    kern = functools.partial(_sgu_kernel, parts=parts, width=width)
    const = lambda i: (0, 0)
    return pl.pallas_call(
        kern,
        out_shape=jax.ShapeDtypeStruct((n, d), F32),
        grid=(n // tm,),
        in_specs=[
            pl.BlockSpec((tm, d), lambda i: (i, 0)),
            pl.BlockSpec((1, d), const),
            pl.BlockSpec((1, d), const),
            pl.BlockSpec((d, 2 * width), const),
            pl.BlockSpec((1, 2 * width), const),
            pl.BlockSpec((1, width), const),
            pl.BlockSpec((1, width), const),
            pl.BlockSpec(w_sp.shape, lambda i: (0, 0, 0)),
            pl.BlockSpec((rows, width), const),
            pl.BlockSpec((width, d), const),
        ],
        out_specs=pl.BlockSpec((tm, d), lambda i: (i, 0)),
        compiler_params=_params(("arbitrary",)),
        name="sgu",
    )(x, g_pre.reshape(1, d), g_post.reshape(1, d), w_uv, b_uv.reshape(1, -1), g_ln.reshape(1, -1),
      b_ln.reshape(1, -1), w_sp, bias, w_out)


def _q_scales():
    da_w, sb_w = DA_HEADS * LANES, SB_PAIRS * LANES
    one = lambda n: jnp.ones((n,), F32)
    return jnp.concatenate([one(da_w) * DA_HEAD_DIM ** -0.5, one(2 * da_w),
                            one(sb_w) * (LOG2E * SB_HEAD_DIM ** -0.5), one(2 * sb_w)])


def kernel(x, g_norm, w_ffn_gate, w_ffn_up, w_ffn_down, w_in_ab, w_out_ab, lambda_params, g_subln,
           w_uv, b_uv, g_sgu_ln, b_sgu_ln, w_spatial, b_spatial, w_out_c):
    b, s, d = x.shape
    depth = g_norm.shape[0]
    assert w_in_ab.shape[2] == 3 * LANES * (DA_HEADS + SB_PAIRS)
    assert g_subln.shape[1] == LANES and w_spatial.shape[1:] == (SGU_GROUPS, SGU_CHUNK, SGU_CHUNK)
    cast = lambda w: w.astype(BF16)
    wg_all, wu_all, wd_all = cast(w_ffn_gate), cast(w_ffn_up), cast(w_ffn_down)
    xf = x.reshape(b * s, d)
    for l in range(depth):
        g = g_norm[l]
        xf = _ffn(xf, g[0], g[1], wg_all, wu_all, wd_all, l, 0)
        if l % 2 == 0:
            e = l // 2
            lambda_init = 0.8 - 0.6 * math.exp(-0.3 * l)
            proj = _inproj(xf, g[2], cast(w_in_ab[e] * _q_scales())).reshape(b, s, -1)
            oa = _diff_attn(proj, lambda_params[e], g_subln[e], lambda_init)
            ob = _sb_attn(proj)
            xf = _outproj(xf, oa.reshape(b * s, -1), ob.reshape(b * s, -1), cast(w_out_ab[e]), g[3])
        else:
            o = l // 2
            xf = _sgu(xf, g[2], g[3], cast(w_uv[o]), b_uv[o], g_sgu_ln[o], b_sgu_ln[o], w_spatial[o],
                      b_spatial[o], cast(w_out_c[o]))
        xf = _ffn(xf, g[4], g[5], wg_all, wu_all, wd_all, l, 1)
    return xf.reshape(b, s, d)
```

```python
import functools
import math

import jax
import jax.numpy as jnp
from jax import lax
from jax.experimental import pallas as pl
from jax.experimental.pallas import tpu as pltpu

F32 = jnp.float32
BF16 = jnp.bfloat16

EPS = 1e-6
LANES = 128
VMEM_LIMIT_BYTES = 52 * 1024 * 1024

DA_HEADS = 4
DA_HEAD_DIM = 64
SB_HEADS = 8
SB_HEAD_DIM = 64
SB_PAIRS = SB_HEADS * SB_HEAD_DIM // LANES
SGU_CHUNK = 128
SGU_GROUPS = 8
ATTN_TILE = 512
SB_BLOCK = 256
LOG2E = math.log2(math.e)
SB_DEAD = -160.0
DIFF_DEAD = 110.0
POS_RADIX = 64


def _pick(n, prefs):
    for p in prefs:
        if n % p == 0:
            return p
    return n


def _rms(x, g):
    return x * lax.rsqrt(jnp.mean(x * x, axis=-1, keepdims=True) + EPS) * g


def _gelu_tanh(x):
    c0 = math.sqrt(2.0 / math.pi)
    inner = x * (c0 + (c0 * 0.044715) * (x * x))
    return (0.5 * x) * (1.0 + jnp.tanh(inner))


def _params(sem):
    return pltpu.CompilerParams(dimension_semantics=sem, vmem_limit_bytes=VMEM_LIMIT_BYTES)


def _ffn_kernel(x_ref, gpre_ref, gpost_ref, wg_ref, wu_ref, wd_ref, o_ref, *, parts, fc):
    rows = x_ref.shape[0] // parts
    sl = [slice(p * rows, (p + 1) * rows) for p in range(parts)]
    hs = [_rms(x_ref[s, :], gpre_ref[...]).astype(BF16) for s in sl]
    accs = [None] * parts
    for c in range(wg_ref.shape[1] // fc):
        cols = slice(c * fc, (c + 1) * fc)
        for p in range(parts):
            gate = jnp.dot(hs[p], wg_ref[:, cols], preferred_element_type=F32)
            up = jnp.dot(hs[p], wu_ref[:, cols], preferred_element_type=F32)
            act = (gate * jax.nn.sigmoid(gate) * up).astype(BF16)
            down = jnp.dot(act, wd_ref[cols, :], preferred_element_type=F32)
            accs[p] = down if accs[p] is None else accs[p] + down
    for p, s in enumerate(sl):
        o_ref[s, :] = x_ref[s, :] + 0.5 * _rms(accs[p], gpost_ref[...])


def _ffn(x, g_pre, g_post, wg, wu, wd, layer, half):
    n, d = x.shape
    f = wg.shape[3]
    tm = _pick(n, (1024, 512, 256, 128))
    fc = _pick(f, (256, 128))
    const = lambda i: (0, 0)
    pick = lambda i: (layer, half, 0, 0)
    return pl.pallas_call(
        functools.partial(_ffn_kernel, parts=2, fc=fc),
        out_shape=jax.ShapeDtypeStruct((n, d), F32),
        grid=(n // tm,),
        in_specs=[
            pl.BlockSpec((tm, d), lambda i: (i, 0)),
            pl.BlockSpec((1, d), const),
            pl.BlockSpec((1, d), const),
            pl.BlockSpec((None, None, d, f), pick),
            pl.BlockSpec((None, None, d, f), pick),
            pl.BlockSpec((None, None, f, d), pick),
        ],
        out_specs=pl.BlockSpec((tm, d), lambda i: (i, 0)),
        compiler_params=_params(("arbitrary",)),
        name="ffn",
    )(x, g_pre.reshape(1, d), g_post.reshape(1, d), wg, wu, wd)


def _inproj_kernel(x_ref, g_ref, w_ref, o_ref, *, parts, nc):
    rows = x_ref.shape[0] // parts
    sl = [slice(p * rows, (p + 1) * rows) for p in range(parts)]
    hs = [_rms(x_ref[s, :], g_ref[...]).astype(BF16) for s in sl]
    for c in range(w_ref.shape[1] // nc):
        cols = slice(c * nc, (c + 1) * nc)
        for p, s in enumerate(sl):
            o_ref[s, cols] = jnp.dot(hs[p], w_ref[:, cols], preferred_element_type=F32).astype(o_ref.dtype)


def _inproj(x, g, w):
    n, d = x.shape
    c = w.shape[1]
    tm = _pick(n, (1024, 512, 256, 128))
    nc = _pick(c, (512, 256, 128))
    return pl.pallas_call(
        functools.partial(_inproj_kernel, parts=2, nc=nc),
        out_shape=jax.ShapeDtypeStruct((n, c), BF16),
        grid=(n // tm,),
        in_specs=[
            pl.BlockSpec((tm, d), lambda i: (i, 0)),
            pl.BlockSpec((1, d), lambda i: (0, 0)),
            pl.BlockSpec((d, c), lambda i: (0, 0)),
        ],
        out_specs=pl.BlockSpec((tm, c), lambda i: (i, 0)),
        compiler_params=_params(("arbitrary",)),
        name="inproj",
    )(x, g.reshape(1, d), w)


def _split_pair(q):
    qf = q.astype(F32)
    lane = lax.broadcasted_iota(jnp.int32, qf.shape, 1)
    lo = jnp.where(lane < DA_HEAD_DIM, qf, 0.0).astype(BF16)
    hi = jnp.where(lane >= DA_HEAD_DIM, qf, 0.0).astype(BF16)
    return lo, hi


def _qk(q, k):
    return lax.dot_general(q, k, (((1,), (1,)), ((), ())), preferred_element_type=F32)


def _diff_kernel(q_ref, k_ref, v_ref, pos_ref, lp_ref, g_ref, o_ref, m_ref, l_ref, a_ref, kmax_ref, live_ref, *, t,
                 lambda_init):
    h = pl.program_id(1)
    qi = pl.program_id(2)
    lane = lax.broadcasted_iota(jnp.int32, (t, LANES), 1)
    head_v = (jnp.zeros((t, LANES), jnp.int32) + (h + 1)).astype(F32)
    slope = jnp.exp2(-8.0 / DA_HEADS * head_v)
    t0 = (jnp.zeros((t, LANES), jnp.int32) + qi * t).astype(F32)
    aug = jnp.where(lane == 0, slope * POS_RADIX,
                    jnp.where(lane == 1, slope, jnp.where(lane == 2, -slope * t0, 0.0))).astype(BF16)
    q_aug = [jnp.concatenate([qm, aug], axis=1) for qm in _split_pair(q_ref[0])]
    ones = jnp.ones((t, LANES), BF16)
    row_i = lax.broadcasted_iota(jnp.int32, (t, t), 0)
    col_i = lax.broadcasted_iota(jnp.int32, (t, t), 1)

    def trip(kjs, masks=None, init=False, next_kj=None):
        starts = [pl.multiple_of(kj * t, t) for kj in kjs]
        ks = [jnp.concatenate([k_ref[0, pl.ds(s0, t), :], pos_ref[pl.ds(s0, t), :]], axis=1) for s0 in starts]
        vs = [jnp.concatenate([v_ref[0, pl.ds(s0, t), :], ones], axis=1) for s0 in starts]
        sss = [[_qk(q_aug[mp], k) for k in ks] for mp in range(2)]
        if masks is not None:
            sss = [[s if mk is None else jnp.where(mk, s, -jnp.inf) for s, mk in zip(ss, masks)] for ss in sss]
        m_news = []
        for mp, ss in enumerate(sss):
            mloc = functools.reduce(jnp.maximum, [jnp.max(s, axis=-1, keepdims=True) for s in ss])
            m_new = jnp.broadcast_to(mloc, (t, LANES))
            m_news.append(m_new if init else jnp.maximum(m_ref[mp], m_new))
        pvs = []
        for ss, m_new in zip(sss, m_news):
            m_wide = jnp.concatenate([m_new] * (t // LANES), axis=1)
            pvs.append(sum(jnp.dot(jnp.exp(s - m_wide).astype(BF16), v, preferred_element_type=F32)
                           for s, v in zip(ss, vs)))
        for mp, (pv, m_new) in enumerate(zip(pvs, m_news)):
            if init:
                a_ref[mp] = pv[:, :LANES]
                l_ref[mp] = pv[:, LANES:]
            else:
                alpha = jnp.exp(m_ref[mp] - m_new)
                a_ref[mp] = alpha * a_ref[mp] + pv[:, :LANES]
                l_ref[mp] = alpha * l_ref[mp] + pv[:, LANES:]
            m_ref[mp] = m_new
        if next_kj is not None:
            live_ref[0] = alive(m_news, next_kj)

    def alive(ms, kj):
        dist = (jnp.zeros((t, LANES), jnp.int32) + (qi - 1 - kj) * t).astype(F32)
        live = jnp.logical_or(reach - ms[0] > slope * dist, reach - ms[1] > slope * dist)
        return jnp.max(jnp.where(live, 1, 0))

    @pl.when(qi == 0)
    def _():
        ksq = [jnp.sum(jnp.square(k_ref[0, c * t:(c + 1) * t, :].astype(F32)), axis=-1, keepdims=True)
               for c in range(k_ref.shape[1] // t)]
        kmax_ref[0] = jnp.sqrt(jnp.max(functools.reduce(jnp.maximum, ksq)))
        live_ref[0] = 0

    qn = jnp.sqrt(jnp.sum(jnp.square(q_ref[0].astype(F32)), axis=-1, keepdims=True))
    reach = jnp.broadcast_to(qn * kmax_ref[0] + DIFF_DEAD, (t, LANES))
    n = jnp.maximum(qi - 1, 0)

    @pl.when(qi == 0)
    def _():
        trip([qi], [col_i <= row_i], init=True)

    @pl.when(qi > 0)
    def _():
        trip([qi, qi - 1], [col_i <= row_i, None], init=True, next_kj=n - 1)

    def body(state):
        i, _ = state
        kj = n - 1 - 2 * i
        nxt = alive(m_ref, kj - 2) > 0
        trip([kj, kj - 1])
        return i + 1, nxt

    _, live = lax.while_loop(lambda st: jnp.logical_and(st[0] < n // 2, st[1]), body, (0, live_ref[0] > 0))

    @pl.when(jnp.logical_and(n % 2 == 1, live))
    def _():
        trip([0])

    lp = lp_ref[...]
    lam = (jnp.exp(jnp.sum(lp[0:1] * lp[1:2], axis=-1, keepdims=True))
           - jnp.exp(jnp.sum(lp[2:3] * lp[3:4], axis=-1, keepdims=True)) + lambda_init)
    oa = a_ref[0] / l_ref[0] - lam * (a_ref[1] / l_ref[1])
    o_ref[0] = (_rms(oa, g_ref[...]) * (1.0 - lambda_init)).astype(o_ref.dtype)


def _key_positions(s):
    pos = jnp.arange(s, dtype=jnp.int32)[:, None]
    lane = jnp.arange(LANES, dtype=jnp.int32)[None, :]
    tab = jnp.where(lane == 0, pos // POS_RADIX, jnp.where(lane == 1, pos % POS_RADIX, (lane == 2).astype(jnp.int32)))
    return tab.astype(BF16)


def _diff_attn(proj, lam_p, g_subln, lambda_init):
    b, s, _ = proj.shape
    t = _pick(s, (ATTN_TILE, LANES))
    assert s // POS_RADIX <= 256 and s // t <= 256
    kern = functools.partial(_diff_kernel, t=t, lambda_init=lambda_init)
    return pl.pallas_call(
        kern,
        out_shape=jax.ShapeDtypeStruct((b, s, DA_HEADS * LANES), BF16),
        grid=(b, DA_HEADS, s // t),
        in_specs=[
            pl.BlockSpec((1, t, LANES), lambda bi, h, qi: (bi, qi, h)),
            pl.BlockSpec((1, s, LANES), lambda bi, h, qi: (bi, 0, DA_HEADS + h)),
            pl.BlockSpec((1, s, LANES), lambda bi, h, qi: (bi, 0, 2 * DA_HEADS + h)),
            pl.BlockSpec((s, LANES), lambda bi, h, qi: (0, 0)),
            pl.BlockSpec(lam_p.shape, lambda bi, h, qi: (0, 0)),
            pl.BlockSpec((1, LANES), lambda bi, h, qi: (0, 0)),
        ],
        out_specs=pl.BlockSpec((1, t, LANES), lambda bi, h, qi: (bi, qi, h)),
        scratch_shapes=[pltpu.VMEM((2, t, LANES), F32)] * 3 + [pltpu.SMEM((1,), F32), pltpu.SMEM((1,), jnp.int32)],
        compiler_params=_params(("arbitrary", "arbitrary", "arbitrary")),
        name="diff_attn",
    )(proj, proj, proj, _key_positions(s), lam_p, g_subln.reshape(1, LANES))


def _sb_kernel(q_ref, k_ref, v_ref, w_ref, o_ref, c_ref, a_ref, live_ref, *, tq, tk):
    qi = pl.program_id(2)
    ratio = tq // tk
    nsub = tk // SB_BLOCK
    qs = _split_pair(q_ref[0])
    row_i = lax.broadcasted_iota(jnp.int32, (tq, SB_BLOCK), 0)
    col_i = lax.broadcasted_iota(jnp.int32, (tq, SB_BLOCK), 1)
    sign = jnp.int32(-2 ** 31)

    def heavy(kjs, diags=None):
        diags = diags or [None] * len(kjs)
        r0s = [dg or 0 for dg in diags]
        starts = [pl.multiple_of(kj * tk, tk) for kj in kjs]
        ks = [k_ref[0, pl.ds(s0, tk), :] for s0 in starts]
        vs = [v_ref[0, pl.ds(s0, tk), :] for s0 in starts]
        wmat = w_ref[...]
        chains = [(ti, hh) for ti in range(len(kjs)) for hh in range(2)]
        zs = [_qk(qs[hh][r0s[ti]:], ks[ti]) for ti, hh in chains]
        later = [None] * len(chains)
        wts = [[None] * nsub for _ in chains]
        for sb in reversed(range(nsub)):
            zbs = [z[:, sb * SB_BLOCK:(sb + 1) * SB_BLOCK] for z in zs]
            stricts = [None if dg is None else (col_i + (dg + sb * SB_BLOCK) < row_i)[dg:] for dg in diags]
            sps = []
            for (ti, hh), zb in zip(chains, zbs):
                neg_abs = lax.bitcast_convert_type(lax.bitcast_convert_type(zb, jnp.int32) | sign, F32)
                sp = jnp.maximum(zb, 0.0) + jnp.log(1.0 + jnp.exp2(neg_abs)) * LOG2E
                sps.append((sp if stricts[ti] is None else jnp.where(stricts[ti], sp, 0.0)).astype(BF16))
            ces = [jnp.dot(sp, wmat, preferred_element_type=F32) for sp in sps]
            for c, (zb, ce) in enumerate(zip(zbs, ces)):
                strict = stricts[chains[c][0]]
                arg = zb + ce
                if later[c] is not None:
                    arg = arg + jnp.concatenate([later[c]] * (SB_BLOCK // LANES), axis=1)
                w = jnp.exp2(arg)
                wts[c][sb] = (w if strict is None else jnp.where(strict, w, 0.0)).astype(BF16)
                total = jnp.broadcast_to(ce[:, 0:1], (ce.shape[0], LANES))
                later[c] = total if later[c] is None else later[c] + total
        pvs = [jnp.dot(wts[c][0] if nsub == 1 else jnp.concatenate(wts[c], axis=1), vs[ti],
                       preferred_element_type=F32) for c, (ti, hh) in enumerate(chains)]
        return [(pv, total, r0s[ti]) for pv, total, (ti, hh) in zip(pvs, later, chains)]

    def combine(outs, init=False):
        carry = [None, None] if init else [c_ref[0], c_ref[1]]
        before = []
        for c, (pv, total, r0) in enumerate(outs):
            hh = c % 2
            before.append(carry[hh])
            if r0:
                total = jnp.concatenate([jnp.zeros((r0, LANES), F32), total], axis=0)
            carry[hh] = total if carry[hh] is None else carry[hh] + total
        c_ref[0], c_ref[1] = carry
        live_ref[0] = jnp.max(jnp.where(jnp.maximum(carry[0], carry[1]) > SB_DEAD, 1, 0))
        for c, (pv, total, r0) in enumerate(outs):
            hh = c % 2
            if before[c] is None:
                if r0:
                    a_ref[hh, :r0] = jnp.zeros((r0, LANES), F32)
                a_ref[hh, r0:] = pv
            else:
                a_ref[hh, r0:] += jnp.exp2(before[c][r0:]) * pv

    n = qi * ratio
    diag_kjs = [n + i for i in reversed(range(ratio))]
    diag_offs = [i * tk for i in reversed(range(ratio))]

    @pl.when(qi == 0)
    def _():
        combine(heavy(diag_kjs, diag_offs), init=True)

    @pl.when(qi > 0)
    def _():
        combine(heavy(diag_kjs + [n - 1], diag_offs + [None]), init=True)

    m = jnp.maximum(n - 1, 0)

    def body(state):
        i, _ = state
        combine(heavy([m - 1 - 2 * i, m - 2 - 2 * i]))
        return i + 1, live_ref[0] > 0

    _, live = lax.while_loop(lambda st: jnp.logical_and(st[0] < m // 2, st[1]), body, (0, live_ref[0] > 0))

    @pl.when(jnp.logical_and(m % 2 == 1, live))
    def _():
        combine(heavy([0]))

    lane = lax.broadcasted_iota(jnp.int32, (tq, LANES), 1)
    o_ref[0] = jnp.where(lane < SB_HEAD_DIM, a_ref[0], a_ref[1]).astype(o_ref.dtype)


def _suffix_matrix():
    j = jnp.arange(SB_BLOCK)[:, None]
    s = jnp.arange(SB_BLOCK)[None, :]
    return -(j >= s).astype(BF16)


def _sb_attn(proj):
    b, s, _ = proj.shape
    t = _pick(s, (ATTN_TILE, SB_BLOCK))
    tk = SB_BLOCK
    base = 3 * DA_HEADS
    kern = functools.partial(_sb_kernel, tq=t, tk=tk)
    return pl.pallas_call(
        kern,
        out_shape=jax.ShapeDtypeStruct((b, s, SB_PAIRS * LANES), BF16),
        grid=(b, SB_PAIRS, s // t),
        in_specs=[
            pl.BlockSpec((1, t, LANES), lambda bi, p, qi: (bi, qi, base + p)),
            pl.BlockSpec((1, s, LANES), lambda bi, p, qi: (bi, 0, base + SB_PAIRS + p)),
            pl.BlockSpec((1, s, LANES), lambda bi, p, qi: (bi, 0, base + 2 * SB_PAIRS + p)),
            pl.BlockSpec((SB_BLOCK, SB_BLOCK), lambda bi, p, qi: (0, 0)),
        ],
        out_specs=pl.BlockSpec((1, t, LANES), lambda bi, p, qi: (bi, qi, p)),
        scratch_shapes=[pltpu.VMEM((2, t, LANES), F32)] * 2 + [pltpu.SMEM((1,), jnp.int32)],
        compiler_params=_params(("arbitrary", "arbitrary", "arbitrary")),
        name="sb_attn",
    )(proj, proj, proj, _suffix_matrix())


def _outproj_kernel(x_ref, oa_ref, ob_ref, wa_ref, wb_ref, g_ref, o_ref, *, parts):
    rows = x_ref.shape[0] // parts
    sl = [slice(p * rows, (p + 1) * rows) for p in range(parts)]
    ms = [jnp.dot(oa_ref[s, :], wa_ref[...], preferred_element_type=F32)
          + jnp.dot(ob_ref[s, :], wb_ref[...], preferred_element_type=F32) for s in sl]
    for s, m in zip(sl, ms):
        o_ref[s, :] = x_ref[s, :] + _rms(m, g_ref[...])


def _outproj(x, oa, ob, w, g):
    n, d = x.shape
    ca, cb = oa.shape[1], ob.shape[1]
    tm = _pick(n, (1024, 512, 256, 128))
    return pl.pallas_call(
        functools.partial(_outproj_kernel, parts=2),
        out_shape=jax.ShapeDtypeStruct((n, d), F32),
        grid=(n // tm,),
        in_specs=[
            pl.BlockSpec((tm, d), lambda i: (i, 0)),
            pl.BlockSpec((tm, ca), lambda i: (i, 0)),
            pl.BlockSpec((tm, cb), lambda i: (i, 0)),
            pl.BlockSpec((ca, d), lambda i: (0, 0)),
            pl.BlockSpec((cb, d), lambda i: (0, 0)),
            pl.BlockSpec((1, d), lambda i: (0, 0)),
        ],
        out_specs=pl.BlockSpec((tm, d), lambda i: (i, 0)),
        compiler_params=_params(("arbitrary",)),
        name="outproj",
    )(x, oa, ob, w[:ca], w[ca:], g.reshape(1, d))


def _sgu_kernel(x_ref, gpre_ref, gpost_ref, wuv_ref, buv_ref, gln_ref, bln_ref, wsp_ref, bsp_ref, wo_ref, o_ref,
                *, parts, width):
    rows = x_ref.shape[0] // parts
    sl = [slice(p * rows, (p + 1) * rows) for p in range(parts)]
    t_i = lax.broadcasted_iota(jnp.int32, (SGU_CHUNK, SGU_CHUNK), 0)
    s_i = lax.broadcasted_iota(jnp.int32, (SGU_CHUNK, SGU_CHUNK), 1)
    gch = width // SGU_GROUPS
    w_tri = [jnp.where(s_i <= t_i, wsp_ref[g], 0.0).astype(BF16) for g in range(SGU_GROUPS)]

    hs = [_rms(x_ref[s, :], gpre_ref[...]).astype(BF16) for s in sl]
    zzs = [jnp.dot(h, wuv_ref[...], preferred_element_type=F32) for h in hs]
    us, vs = [], []
    for zz in zzs:
        zz = _gelu_tanh(zz + buv_ref[...])
        v = zz[:, width:]
        vc = v - jnp.mean(v, axis=-1, keepdims=True)
        vs.append((vc * lax.rsqrt(jnp.mean(vc * vc, axis=-1, keepdims=True) + EPS) * gln_ref[...]
                   + bln_ref[...]).astype(BF16))
        us.append(zz[:, :width])
    mixeds = [jnp.concatenate(
        [jnp.concatenate([jnp.dot(w_tri[g], v[n * SGU_CHUNK:(n + 1) * SGU_CHUNK, g * gch:(g + 1) * gch],
                                  preferred_element_type=F32) for n in range(rows // SGU_CHUNK)], axis=0)
         for g in range(SGU_GROUPS)], axis=1) for v in vs]
    ys = [jnp.dot((u * (mixed + bsp_ref[...])).astype(BF16), wo_ref[...], preferred_element_type=F32)
          for u, mixed in zip(us, mixeds)]
    for s, y in zip(sl, ys):
        o_ref[s, :] = x_ref[s, :] + _rms(y, gpost_ref[...])


def _sgu(x, g_pre, g_post, w_uv, b_uv, g_ln, b_ln, w_sp, b_sp, w_out):
    n, d = x.shape
    width = w_out.shape[0]
    tm = _pick(n, (1024, 512, 256))
    parts = 2
    rows = tm // parts
    gch = width // SGU_GROUPS
    bias = jnp.tile(jnp.repeat(b_sp.T, gch, axis=1), (rows // SGU_CHUNK, 1))
    kern = functools.partial(_sgu_kernel, parts=parts, width=width)
    const = lambda i: (0, 0)
    return pl.pallas_call(
        kern,
        out_shape=jax.ShapeDtypeStruct((n, d), F32),
        grid=(n // tm,),
        in_specs=[
            pl.BlockSpec((tm, d), lambda i: (i, 0)),
            pl.BlockSpec((1, d), const),
            pl.BlockSpec((1, d), const),
            pl.BlockSpec((d, 2 * width), const),
            pl.BlockSpec((1, 2 * width), const),
            pl.BlockSpec((1, width), const),
            pl.BlockSpec((1, width), const),
            pl.BlockSpec(w_sp.shape, lambda i: (0, 0, 0)),
            pl.BlockSpec((rows, width), const),
            pl.BlockSpec((width, d), const),
        ],
        out_specs=pl.BlockSpec((tm, d), lambda i: (i, 0)),
        compiler_params=_params(("arbitrary",)),
        name="sgu",
    )(x, g_pre.reshape(1, d), g_post.reshape(1, d), w_uv, b_uv.reshape(1, -1), g_ln.reshape(1, -1),
      b_ln.reshape(1, -1), w_sp, bias, w_out)


def _q_scales():
    da_w, sb_w = DA_HEADS * LANES, SB_PAIRS * LANES
    one = lambda n: jnp.ones((n,), F32)
    return jnp.concatenate([one(da_w) * DA_HEAD_DIM ** -0.5, one(2 * da_w),
                            one(sb_w) * (LOG2E * SB_HEAD_DIM ** -0.5), one(2 * sb_w)])


def kernel(x, g_norm, w_ffn_gate, w_ffn_up, w_ffn_down, w_in_ab, w_out_ab, lambda_params, g_subln,
           w_uv, b_uv, g_sgu_ln, b_sgu_ln, w_spatial, b_spatial, w_out_c):
    b, s, d = x.shape
    depth = g_norm.shape[0]
    assert w_in_ab.shape[2] == 3 * LANES * (DA_HEADS + SB_PAIRS)
    assert g_subln.shape[1] == LANES and w_spatial.shape[1:] == (SGU_GROUPS, SGU_CHUNK, SGU_CHUNK)
    cast = lambda w: w.astype(BF16)
    wg_all, wu_all, wd_all = cast(w_ffn_gate), cast(w_ffn_up), cast(w_ffn_down)
    xf = x.reshape(b * s, d)
    for l in range(depth):
        g = g_norm[l]
        xf = _ffn(xf, g[0], g[1], wg_all, wu_all, wd_all, l, 0)
        if l % 2 == 0:
            e = l // 2
            lambda_init = 0.8 - 0.6 * math.exp(-0.3 * l)
            proj = _inproj(xf, g[2], cast(w_in_ab[e] * _q_scales())).reshape(b, s, -1)
            oa = _diff_attn(proj, lambda_params[e], g_subln[e], lambda_init)
            ob = _sb_attn(proj)
            xf = _outproj(xf, oa.reshape(b * s, -1), ob.reshape(b * s, -1), cast(w_out_ab[e]), g[3])
        else:
            o = l // 2
            xf = _sgu(xf, g[2], g[3], cast(w_uv[o]), b_uv[o], g_sgu_ln[o], b_sgu_ln[o], w_spatial[o],
                      b_spatial[o], cast(w_out_c[o]))
        xf = _ffn(xf, g[4], g[5], wg_all, wu_all, wd_all, l, 1)
    return xf.reshape(b, s, d)
```

```python
import functools
import math

import jax
import jax.numpy as jnp
from jax import lax
from jax.experimental import pallas as pl
from jax.experimental.pallas import tpu as pltpu

F32 = jnp.float32
BF16 = jnp.bfloat16

EPS = 1e-6
LANES = 128
VMEM_LIMIT_BYTES = 52 * 1024 * 1024

DA_HEADS = 4
DA_HEAD_DIM = 64
SB_HEADS = 8
SB_HEAD_DIM = 64
SB_PAIRS = SB_HEADS * SB_HEAD_DIM // LANES
SGU_CHUNK = 128
SGU_GROUPS = 8
ATTN_TILE = 512
SB_BLOCK = 256
LOG2E = math.log2(math.e)
SB_DEAD = -160.0
DIFF_DEAD = 110.0
POS_RADIX = 64


def _pick(n, prefs):
    for p in prefs:
        if n % p == 0:
            return p
    return n


def _rms(x, g):
    return x * lax.rsqrt(jnp.mean(x * x, axis=-1, keepdims=True) + EPS) * g


def _gelu_tanh(x):
    c0 = math.sqrt(2.0 / math.pi)
    inner = x * (c0 + (c0 * 0.044715) * (x * x))
    return (0.5 * x) * (1.0 + jnp.tanh(inner))


def _params(sem):
    return pltpu.CompilerParams(dimension_semantics=sem, vmem_limit_bytes=VMEM_LIMIT_BYTES)


def _ffn_kernel(x_ref, gpre_ref, gpost_ref, wg_ref, wu_ref, wd_ref, o_ref, *, parts, fc):
    rows = x_ref.shape[0] // parts
    sl = [slice(p * rows, (p + 1) * rows) for p in range(parts)]
    hs = [_rms(x_ref[s, :], gpre_ref[...]).astype(BF16) for s in sl]
    accs = [None] * parts
    for c in range(wg_ref.shape[1] // fc):
        cols = slice(c * fc, (c + 1) * fc)
        for p in range(parts):
            gate = jnp.dot(hs[p], wg_ref[:, cols], preferred_element_type=F32)
            up = jnp.dot(hs[p], wu_ref[:, cols], preferred_element_type=F32)
            act = (gate * jax.nn.sigmoid(gate) * up).astype(BF16)
            down = jnp.dot(act, wd_ref[cols, :], preferred_element_type=F32)
            accs[p] = down if accs[p] is None else accs[p] + down
    for p, s in enumerate(sl):
        o_ref[s, :] = x_ref[s, :] + 0.5 * _rms(accs[p], gpost_ref[...])


def _ffn(x, g_pre, g_post, wg, wu, wd, layer, half):
    n, d = x.shape
    f = wg.shape[3]
    tm = _pick(n, (1024, 512, 256, 128))
    fc = _pick(f, (256, 128))
    const = lambda i: (0, 0)
    pick = lambda i: (layer, half, 0, 0)
    return pl.pallas_call(
        functools.partial(_ffn_kernel, parts=2, fc=fc),
        out_shape=jax.ShapeDtypeStruct((n, d), F32),
        grid=(n // tm,),
        in_specs=[
            pl.BlockSpec((tm, d), lambda i: (i, 0)),
            pl.BlockSpec((1, d), const),
            pl.BlockSpec((1, d), const),
            pl.BlockSpec((None, None, d, f), pick),
            pl.BlockSpec((None, None, d, f), pick),
            pl.BlockSpec((None, None, f, d), pick),
        ],
        out_specs=pl.BlockSpec((tm, d), lambda i: (i, 0)),
        compiler_params=_params(("arbitrary",)),
        name="ffn",
    )(x, g_pre.reshape(1, d), g_post.reshape(1, d), wg, wu, wd)


def _inproj_kernel(x_ref, g_ref, w_ref, o_ref, *, parts, nc):
    rows = x_ref.shape[0] // parts
    sl = [slice(p * rows, (p + 1) * rows) for p in range(parts)]
    hs = [_rms(x_ref[s, :], g_ref[...]).astype(BF16) for s in sl]
    for c in range(w_ref.shape[1] // nc):
        cols = slice(c * nc, (c + 1) * nc)
        for p, s in enumerate(sl):
            o_ref[s, cols] = jnp.dot(hs[p], w_ref[:, cols], preferred_element_type=F32).astype(o_ref.dtype)


def _inproj(x, g, w):
    n, d = x.shape
    c = w.shape[1]
    tm = _pick(n, (1024, 512, 256, 128))
    nc = _pick(c, (512, 256, 128))
    return pl.pallas_call(
        functools.partial(_inproj_kernel, parts=2, nc=nc),
        out_shape=jax.ShapeDtypeStruct((n, c), BF16),
        grid=(n // tm,),
        in_specs=[
            pl.BlockSpec((tm, d), lambda i: (i, 0)),
            pl.BlockSpec((1, d), lambda i: (0, 0)),
            pl.BlockSpec((d, c), lambda i: (0, 0)),
        ],
        out_specs=pl.BlockSpec((tm, c), lambda i: (i, 0)),
        compiler_params=_params(("arbitrary",)),
        name="inproj",
    )(x, g.reshape(1, d), w)


def _split_pair(q):
    qf = q.astype(F32)
    lane = lax.broadcasted_iota(jnp.int32, qf.shape, 1)
    lo = jnp.where(lane < DA_HEAD_DIM, qf, 0.0).astype(BF16)
    hi = jnp.where(lane >= DA_HEAD_DIM, qf, 0.0).astype(BF16)
    return lo, hi


def _qk(q, k):
    return lax.dot_general(q, k, (((1,), (1,)), ((), ())), preferred_element_type=F32)


def _diff_kernel(q_ref, k_ref, v_ref, pos_ref, lp_ref, g_ref, o_ref, m_ref, l_ref, a_ref, kmax_ref, live_ref, *, t,
                 lambda_init):
    h = pl.program_id(1)
    qi = pl.program_id(2)
    lane = lax.broadcasted_iota(jnp.int32, (t, LANES), 1)
    head_v = (jnp.zeros((t, LANES), jnp.int32) + (h + 1)).astype(F32)
    slope = jnp.exp2(-8.0 / DA_HEADS * head_v)
    t0 = (jnp.zeros((t, LANES), jnp.int32) + qi * t).astype(F32)
    aug = jnp.where(lane == 0, slope * POS_RADIX,
                    jnp.where(lane == 1, slope, jnp.where(lane == 2, -slope * t0, 0.0))).astype(BF16)
    q_aug = [jnp.concatenate([qm, aug], axis=1) for qm in _split_pair(q_ref[0])]
    ones = jnp.ones((t, LANES), BF16)
    row_i = lax.broadcasted_iota(jnp.int32, (t, t), 0)
    col_i = lax.broadcasted_iota(jnp.int32, (t, t), 1)

    def trip(kjs, masks=None, init=False, next_kj=None):
        starts = [pl.multiple_of(kj * t, t) for kj in kjs]
        ks = [jnp.concatenate([k_ref[0, pl.ds(s0, t), :], pos_ref[pl.ds(s0, t), :]], axis=1) for s0 in starts]
        vs = [jnp.concatenate([v_ref[0, pl.ds(s0, t), :], ones], axis=1) for s0 in starts]
        sss = [[_qk(q_aug[mp], k) for k in ks] for mp in range(2)]
        if masks is not None:
            sss = [[s if mk is None else jnp.where(mk, s, -jnp.inf) for s, mk in zip(ss, masks)] for ss in sss]
        m_news = []
        for mp, ss in enumerate(sss):
            mloc = functools.reduce(jnp.maximum, [jnp.max(s, axis=-1, keepdims=True) for s in ss])
            m_new = jnp.broadcast_to(mloc, (t, LANES))
            m_news.append(m_new if init else jnp.maximum(m_ref[mp], m_new))
        pvs = []
        for ss, m_new in zip(sss, m_news):
            m_wide = jnp.concatenate([m_new] * (t // LANES), axis=1)
            pvs.append(sum(jnp.dot(jnp.exp(s - m_wide).astype(BF16), v, preferred_element_type=F32)
                           for s, v in zip(ss, vs)))
        for mp, (pv, m_new) in enumerate(zip(pvs, m_news)):
            if init:
                a_ref[mp] = pv[:, :LANES]
                l_ref[mp] = pv[:, LANES:]
            else:
                alpha = jnp.exp(m_ref[mp] - m_new)
                a_ref[mp] = alpha * a_ref[mp] + pv[:, :LANES]
                l_ref[mp] = alpha * l_ref[mp] + pv[:, LANES:]
            m_ref[mp] = m_new
        if next_kj is not None:
            live_ref[0] = alive(m_news, next_kj)
            live_ref[1] = alive(m_news, next_kj - 2)

    def alive(ms, kj):
        dist = (jnp.zeros((t, LANES), jnp.int32) + (qi - 1 - kj) * t).astype(F32)
        live = jnp.logical_or(reach - ms[0] > slope * dist, reach - ms[1] > slope * dist)
        return jnp.max(jnp.where(live, 1, 0))

    @pl.when(qi == 0)
    def _():
        ksq = [jnp.sum(jnp.square(k_ref[0, c * t:(c + 1) * t, :].astype(F32)), axis=-1, keepdims=True)
               for c in range(k_ref.shape[1] // t)]
        kmax_ref[0] = jnp.sqrt(jnp.max(functools.reduce(jnp.maximum, ksq)))
        live_ref[0] = 0
        live_ref[1] = 0

    qn = jnp.sqrt(jnp.sum(jnp.square(q_ref[0].astype(F32)), axis=-1, keepdims=True))
    reach = jnp.broadcast_to(qn * kmax_ref[0] + DIFF_DEAD, (t, LANES))
    n = jnp.maximum(qi - 1, 0)

    @pl.when(qi == 0)
    def _():
        trip([qi], [col_i <= row_i], init=True)

    @pl.when(qi > 0)
    def _():
        trip([qi, qi - 1], [col_i <= row_i, None], init=True, next_kj=n - 1)

    def quads(state):
        kj, _, _ = state
        nxt2, nxt4 = alive(m_ref, kj - 4) > 0, alive(m_ref, kj - 6) > 0
        trip([kj, kj - 1])
        trip([kj - 2, kj - 3])
        return kj - 4, nxt2, nxt4

    kj, live, _ = lax.while_loop(lambda st: jnp.logical_and(st[0] >= 3, st[2]), quads,
                                 (n - 1, live_ref[0] > 0, live_ref[1] > 0))

    def pairs(state):
        kj, _ = state
        nxt = alive(m_ref, kj - 2) > 0
        trip([kj, kj - 1])
        return kj - 2, nxt

    kj, live = lax.while_loop(lambda st: jnp.logical_and(st[0] >= 1, st[1]), pairs, (kj, live))

    @pl.when(jnp.logical_and(kj == 0, live))
    def _():
        trip([0])

    lp = lp_ref[...]
    lam = (jnp.exp(jnp.sum(lp[0:1] * lp[1:2], axis=-1, keepdims=True))
           - jnp.exp(jnp.sum(lp[2:3] * lp[3:4], axis=-1, keepdims=True)) + lambda_init)
    oa = a_ref[0] / l_ref[0] - lam * (a_ref[1] / l_ref[1])
    o_ref[0] = (_rms(oa, g_ref[...]) * (1.0 - lambda_init)).astype(o_ref.dtype)


def _key_positions(s):
    pos = jnp.arange(s, dtype=jnp.int32)[:, None]
    lane = jnp.arange(LANES, dtype=jnp.int32)[None, :]
    tab = jnp.where(lane == 0, pos // POS_RADIX, jnp.where(lane == 1, pos % POS_RADIX, (lane == 2).astype(jnp.int32)))
    return tab.astype(BF16)


def _diff_attn(proj, lam_p, g_subln, lambda_init):
    b, s, _ = proj.shape
    t = _pick(s, (ATTN_TILE, LANES))
    assert s // POS_RADIX <= 256 and s // t <= 256
    kern = functools.partial(_diff_kernel, t=t, lambda_init=lambda_init)
    return pl.pallas_call(
        kern,
        out_shape=jax.ShapeDtypeStruct((b, s, DA_HEADS * LANES), BF16),
        grid=(b, DA_HEADS, s // t),
        in_specs=[
            pl.BlockSpec((1, t, LANES), lambda bi, h, qi: (bi, qi, h)),
            pl.BlockSpec((1, s, LANES), lambda bi, h, qi: (bi, 0, DA_HEADS + h)),
            pl.BlockSpec((1, s, LANES), lambda bi, h, qi: (bi, 0, 2 * DA_HEADS + h)),
            pl.BlockSpec((s, LANES), lambda bi, h, qi: (0, 0)),
            pl.BlockSpec(lam_p.shape, lambda bi, h, qi: (0, 0)),
            pl.BlockSpec((1, LANES), lambda bi, h, qi: (0, 0)),
        ],
        out_specs=pl.BlockSpec((1, t, LANES), lambda bi, h, qi: (bi, qi, h)),
        scratch_shapes=[pltpu.VMEM((2, t, LANES), F32)] * 3 + [pltpu.SMEM((1,), F32), pltpu.SMEM((2,), jnp.int32)],
        compiler_params=_params(("arbitrary", "arbitrary", "arbitrary")),
        name="diff_attn",
    )(proj, proj, proj, _key_positions(s), lam_p, g_subln.reshape(1, LANES))


def _sb_kernel(q_ref, k_ref, v_ref, w_ref, o_ref, c_ref, a_ref, live_ref, *, tq, tk):
    qi = pl.program_id(2)
    ratio = tq // tk
    nsub = tk // SB_BLOCK
    qs = _split_pair(q_ref[0])
    row_i = lax.broadcasted_iota(jnp.int32, (tq, SB_BLOCK), 0)
    col_i = lax.broadcasted_iota(jnp.int32, (tq, SB_BLOCK), 1)
    sign = jnp.int32(-2 ** 31)

    def heavy(kjs, diags=None):
        diags = diags or [None] * len(kjs)
        r0s = [dg or 0 for dg in diags]
        starts = [pl.multiple_of(kj * tk, tk) for kj in kjs]
        ks = [k_ref[0, pl.ds(s0, tk), :] for s0 in starts]
        vs = [v_ref[0, pl.ds(s0, tk), :] for s0 in starts]
        wmat = w_ref[...]
        chains = [(ti, hh) for ti in range(len(kjs)) for hh in range(2)]
        zs = [_qk(qs[hh][r0s[ti]:], ks[ti]) for ti, hh in chains]
        later = [None] * len(chains)
        wts = [[None] * nsub for _ in chains]
        for sb in reversed(range(nsub)):
            zbs = [z[:, sb * SB_BLOCK:(sb + 1) * SB_BLOCK] for z in zs]
            stricts = [None if dg is None else (col_i + (dg + sb * SB_BLOCK) < row_i)[dg:] for dg in diags]
            sps = []
            for (ti, hh), zb in zip(chains, zbs):
                neg_abs = lax.bitcast_convert_type(lax.bitcast_convert_type(zb, jnp.int32) | sign, F32)
                sp = jnp.maximum(zb, 0.0) + jnp.log(1.0 + jnp.exp2(neg_abs)) * LOG2E
                sps.append((sp if stricts[ti] is None else jnp.where(stricts[ti], sp, 0.0)).astype(BF16))
            ces = [jnp.dot(sp, wmat, preferred_element_type=F32) for sp in sps]
            for c, (zb, ce) in enumerate(zip(zbs, ces)):
                strict = stricts[chains[c][0]]
                arg = zb + ce
                if later[c] is not None:
                    arg = arg + jnp.concatenate([later[c]] * (SB_BLOCK // LANES), axis=1)
                w = jnp.exp2(arg)
                wts[c][sb] = (w if strict is None else jnp.where(strict, w, 0.0)).astype(BF16)
                total = jnp.broadcast_to(ce[:, 0:1], (ce.shape[0], LANES))
                later[c] = total if later[c] is None else later[c] + total
        pvs = [jnp.dot(wts[c][0] if nsub == 1 else jnp.concatenate(wts[c], axis=1), vs[ti],
                       preferred_element_type=F32) for c, (ti, hh) in enumerate(chains)]
        return [(pv, total, r0s[ti]) for pv, total, (ti, hh) in zip(pvs, later, chains)]

    def combine(outs, init=False):
        carry = [None, None] if init else [c_ref[0], c_ref[1]]
        before = []
        for c, (pv, total, r0) in enumerate(outs):
            hh = c % 2
            before.append(carry[hh])
            if r0:
                total = jnp.concatenate([jnp.zeros((r0, LANES), F32), total], axis=0)
            carry[hh] = total if carry[hh] is None else carry[hh] + total
        c_ref[0], c_ref[1] = carry
        live_ref[0] = jnp.max(jnp.where(jnp.maximum(carry[0], carry[1]) > SB_DEAD, 1, 0))
        for c, (pv, total, r0) in enumerate(outs):
            hh = c % 2
            if before[c] is None:
                if r0:
                    a_ref[hh, :r0] = jnp.zeros((r0, LANES), F32)
                a_ref[hh, r0:] = pv
            else:
                a_ref[hh, r0:] += jnp.exp2(before[c][r0:]) * pv

    n = qi * ratio
    diag_kjs = [n + i for i in reversed(range(ratio))]
    diag_offs = [i * tk for i in reversed(range(ratio))]

    @pl.when(qi == 0)
    def _():
        combine(heavy(diag_kjs, diag_offs), init=True)

    @pl.when(qi > 0)
    def _():
        combine(heavy(diag_kjs + [n - 1], diag_offs + [None]), init=True)

    m = jnp.maximum(n - 1, 0)

    def body(state):
        i, _ = state
        combine(heavy([m - 1 - 2 * i, m - 2 - 2 * i]))
        return i + 1, live_ref[0] > 0

    _, live = lax.while_loop(lambda st: jnp.logical_and(st[0] < m // 2, st[1]), body, (0, live_ref[0] > 0))

    @pl.when(jnp.logical_and(m % 2 == 1, live))
    def _():
        combine(heavy([0]))

    lane = lax.broadcasted_iota(jnp.int32, (tq, LANES), 1)
    o_ref[0] = jnp.where(lane < SB_HEAD_DIM, a_ref[0], a_ref[1]).astype(o_ref.dtype)


def _suffix_matrix():
    j = jnp.arange(SB_BLOCK)[:, None]
    s = jnp.arange(SB_BLOCK)[None, :]
    return -(j >= s).astype(BF16)


def _sb_attn(proj):
    b, s, _ = proj.shape
    t = _pick(s, (ATTN_TILE, SB_BLOCK))
    tk = SB_BLOCK
    base = 3 * DA_HEADS
    kern = functools.partial(_sb_kernel, tq=t, tk=tk)
    return pl.pallas_call(
        kern,
        out_shape=jax.ShapeDtypeStruct((b, s, SB_PAIRS * LANES), BF16),
        grid=(b, SB_PAIRS, s // t),
        in_specs=[
            pl.BlockSpec((1, t, LANES), lambda bi, p, qi: (bi, qi, base + p)),
            pl.BlockSpec((1, s, LANES), lambda bi, p, qi: (bi, 0, base + SB_PAIRS + p)),
            pl.BlockSpec((1, s, LANES), lambda bi, p, qi: (bi, 0, base + 2 * SB_PAIRS + p)),
            pl.BlockSpec((SB_BLOCK, SB_BLOCK), lambda bi, p, qi: (0, 0)),
        ],
        out_specs=pl.BlockSpec((1, t, LANES), lambda bi, p, qi: (bi, qi, p)),
        scratch_shapes=[pltpu.VMEM((2, t, LANES), F32)] * 2 + [pltpu.SMEM((1,), jnp.int32)],
        compiler_params=_params(("arbitrary", "arbitrary", "arbitrary")),
        name="sb_attn",
    )(proj, proj, proj, _suffix_matrix())


def _outproj_kernel(x_ref, oa_ref, ob_ref, wa_ref, wb_ref, g_ref, o_ref, *, parts):
    rows = x_ref.shape[0] // parts
    sl = [slice(p * rows, (p + 1) * rows) for p in range(parts)]
    ms = [jnp.dot(oa_ref[s, :], wa_ref[...], preferred_element_type=F32)
          + jnp.dot(ob_ref[s, :], wb_ref[...], preferred_element_type=F32) for s in sl]
    for s, m in zip(sl, ms):
        o_ref[s, :] = x_ref[s, :] + _rms(m, g_ref[...])


def _outproj(x, oa, ob, w, g):
    n, d = x.shape
    ca, cb = oa.shape[1], ob.shape[1]
    tm = _pick(n, (1024, 512, 256, 128))
    return pl.pallas_call(
        functools.partial(_outproj_kernel, parts=2),
        out_shape=jax.ShapeDtypeStruct((n, d), F32),
        grid=(n // tm,),
        in_specs=[
            pl.BlockSpec((tm, d), lambda i: (i, 0)),
            pl.BlockSpec((tm, ca), lambda i: (i, 0)),
            pl.BlockSpec((tm, cb), lambda i: (i, 0)),
            pl.BlockSpec((ca, d), lambda i: (0, 0)),
            pl.BlockSpec((cb, d), lambda i: (0, 0)),
            pl.BlockSpec((1, d), lambda i: (0, 0)),
        ],
        out_specs=pl.BlockSpec((tm, d), lambda i: (i, 0)),
        compiler_params=_params(("arbitrary",)),
        name="outproj",
    )(x, oa, ob, w[:ca], w[ca:], g.reshape(1, d))


def _sgu_kernel(x_ref, gpre_ref, gpost_ref, wuv_ref, buv_ref, gln_ref, bln_ref, wsp_ref, bsp_ref, wo_ref, o_ref,
                *, parts, width):
    rows = x_ref.shape[0] // parts
    sl = [slice(p * rows, (p + 1) * rows) for p in range(parts)]
    t_i = lax.broadcasted_iota(jnp.int32, (SGU_CHUNK, SGU_CHUNK), 0)
    s_i = lax.broadcasted_iota(jnp.int32, (SGU_CHUNK, SGU_CHUNK), 1)
    gch = width // SGU_GROUPS
    w_tri = [jnp.where(s_i <= t_i, wsp_ref[g], 0.0).astype(BF16) for g in range(SGU_GROUPS)]

    hs = [_rms(x_ref[s, :], gpre_ref[...]).astype(BF16) for s in sl]
    zzs = [jnp.dot(h, wuv_ref[...], preferred_element_type=F32) for h in hs]
    us, vs = [], []
    for zz in zzs:
        zz = _gelu_tanh(zz + buv_ref[...])
        v = zz[:, width:]
        vc = v - jnp.mean(v, axis=-1, keepdims=True)
        vs.append((vc * lax.rsqrt(jnp.mean(vc * vc, axis=-1, keepdims=True) + EPS) * gln_ref[...]
                   + bln_ref[...]).astype(BF16))
        us.append(zz[:, :width])
    mixeds = [jnp.concatenate(
        [jnp.concatenate([jnp.dot(w_tri[g], v[n * SGU_CHUNK:(n + 1) * SGU_CHUNK, g * gch:(g + 1) * gch],
                                  preferred_element_type=F32) for n in range(rows // SGU_CHUNK)], axis=0)
         for g in range(SGU_GROUPS)], axis=1) for v in vs]
    ys = [jnp.dot((u * (mixed + bsp_ref[...])).astype(BF16), wo_ref[...], preferred_element_type=F32)
          for u, mixed in zip(us, mixeds)]
    for s, y in zip(sl, ys):
        o_ref[s, :] = x_ref[s, :] + _rms(y, gpost_ref[...])


def _sgu(x, g_pre, g_post, w_uv, b_uv, g_ln, b_ln, w_sp, b_sp, w_out):
    n, d = x.shape
    width = w_out.shape[0]
    tm = _pick(n, (1024, 512, 256))
    parts = 2
    rows = tm // parts
    gch = width // SGU_GROUPS
    bias = jnp.tile(jnp.repeat(b_sp.T, gch, axis=1), (rows // SGU_CHUNK, 1))
    kern = functools.partial(_sgu_kernel, parts=parts, width=width)
    const = lambda i: (0, 0)
    return pl.pallas_call(
        kern,
        out_shape=jax.ShapeDtypeStruct((n, d), F32),
        grid=(n // tm,),
        in_specs=[
            pl.BlockSpec((tm, d), lambda i: (i, 0)),
            pl.BlockSpec((1, d), const),
            pl.BlockSpec((1, d), const),
            pl.BlockSpec((d, 2 * width), const),
            pl.BlockSpec((1, 2 * width), const),
            pl.BlockSpec((1, width), const),
            pl.BlockSpec((1, width), const),
            pl.BlockSpec(w_sp.shape, lambda i: (0, 0, 0)),
            pl.BlockSpec((rows, width), const),
            pl.BlockSpec((width, d), const),
        ],
        out_specs=pl.BlockSpec((tm, d), lambda i: (i, 0)),
        compiler_params=_params(("arbitrary",)),
        name="sgu",
    )(x, g_pre.reshape(1, d), g_post.reshape(1, d), w_uv, b_uv.reshape(1, -1), g_ln.reshape(1, -1),
      b_ln.reshape(1, -1), w_sp, bias, w_out)


def _q_scales():
    da_w, sb_w = DA_HEADS * LANES, SB_PAIRS * LANES
    one = lambda n: jnp.ones((n,), F32)
    return jnp.concatenate([one(da_w) * DA_HEAD_DIM ** -0.5, one(2 * da_w),
                            one(sb_w) * (LOG2E * SB_HEAD_DIM ** -0.5), one(2 * sb_w)])


def kernel(x, g_norm, w_ffn_gate, w_ffn_up, w_ffn_down, w_in_ab, w_out_ab, lambda_params, g_subln,
           w_uv, b_uv, g_sgu_ln, b_sgu_ln, w_spatial, b_spatial, w_out_c):
    b, s, d = x.shape
    depth = g_norm.shape[0]
    assert w_in_ab.shape[2] == 3 * LANES * (DA_HEADS + SB_PAIRS)
    assert g_subln.shape[1] == LANES and w_spatial.shape[1:] == (SGU_GROUPS, SGU_CHUNK, SGU_CHUNK)
    cast = lambda w: w.astype(BF16)
    wg_all, wu_all, wd_all = cast(w_ffn_gate), cast(w_ffn_up), cast(w_ffn_down)
    xf = x.reshape(b * s, d)
    for l in range(depth):
        g = g_norm[l]
        xf = _ffn(xf, g[0], g[1], wg_all, wu_all, wd_all, l, 0)
        if l % 2 == 0:
            e = l // 2
            lambda_init = 0.8 - 0.6 * math.exp(-0.3 * l)
            proj = _inproj(xf, g[2], cast(w_in_ab[e] * _q_scales())).reshape(b, s, -1)
            oa = _diff_attn(proj, lambda_params[e], g_subln[e], lambda_init)
            ob = _sb_attn(proj)
            xf = _outproj(xf, oa.reshape(b * s, -1), ob.reshape(b * s, -1), cast(w_out_ab[e]), g[3])
        else:
            o = l // 2
            xf = _sgu(xf, g[2], g[3], cast(w_uv[o]), b_uv[o], g_sgu_ln[o], b_sgu_ln[o], w_spatial[o],
                      b_spatial[o], cast(w_out_c[o]))
        xf = _ffn(xf, g[4], g[5], wg_all, wu_all, wd_all, l, 1)
    return xf.reshape(b, s, d)
```

```python
import functools
import math

import jax
import jax.numpy as jnp
from jax import lax
from jax.experimental import pallas as pl
from jax.experimental.pallas import tpu as pltpu

F32 = jnp.float32
BF16 = jnp.bfloat16

EPS = 1e-6
LANES = 128
VMEM_LIMIT_BYTES = 52 * 1024 * 1024

DA_HEADS = 4
DA_HEAD_DIM = 64
SB_HEADS = 8
SB_HEAD_DIM = 64
SB_PAIRS = SB_HEADS * SB_HEAD_DIM // LANES
SGU_CHUNK = 128
SGU_GROUPS = 8
ATTN_TILE = 512
SB_BLOCK = 256
LOG2E = math.log2(math.e)
SB_DEAD = -160.0
DIFF_DEAD = 110.0
POS_RADIX = 64


def _pick(n, prefs):
    for p in prefs:
        if n % p == 0:
            return p
    return n


def _rms(x, g):
    return x * lax.rsqrt(jnp.mean(x * x, axis=-1, keepdims=True) + EPS) * g


def _gelu_tanh(x):
    c0 = math.sqrt(2.0 / math.pi)
    inner = x * (c0 + (c0 * 0.044715) * (x * x))
    return (0.5 * x) * (1.0 + jnp.tanh(inner))


def _params(sem):
    return pltpu.CompilerParams(dimension_semantics=sem, vmem_limit_bytes=VMEM_LIMIT_BYTES)


def _ffn_kernel(*refs, parts, fc, mixer):
    if mixer:
        x_ref, oa_ref, ob_ref, wa_ref, wb_ref, gmix_ref, gpre_ref, gpost_ref, wg_ref, wu_ref, wd_ref, o_ref = refs
    else:
        x_ref, gpre_ref, gpost_ref, wg_ref, wu_ref, wd_ref, o_ref = refs
    rows = x_ref.shape[0] // parts
    sl = [slice(p * rows, (p + 1) * rows) for p in range(parts)]
    ys = [x_ref[s, :] for s in sl]
    if mixer:
        mixes = [jnp.dot(oa_ref[s, :], wa_ref[...], preferred_element_type=F32)
                 + jnp.dot(ob_ref[s, :], wb_ref[...], preferred_element_type=F32) for s in sl]
        ys = [y + _rms(m, gmix_ref[...]) for y, m in zip(ys, mixes)]
    hs = [_rms(y, gpre_ref[...]).astype(BF16) for y in ys]
    accs = [None] * parts
    for c in range(wg_ref.shape[1] // fc):
        cols = slice(c * fc, (c + 1) * fc)
        for p in range(parts):
            gate = jnp.dot(hs[p], wg_ref[:, cols], preferred_element_type=F32)
            up = jnp.dot(hs[p], wu_ref[:, cols], preferred_element_type=F32)
            act = (gate * jax.nn.sigmoid(gate) * up).astype(BF16)
            down = jnp.dot(act, wd_ref[cols, :], preferred_element_type=F32)
            accs[p] = down if accs[p] is None else accs[p] + down
    for p, s in enumerate(sl):
        o_ref[s, :] = ys[p] + 0.5 * _rms(accs[p], gpost_ref[...])


def _ffn(x, g_pre, g_post, wg, wu, wd, layer, half, mixer=None):
    n, d = x.shape
    f = wg.shape[3]
    tm = _pick(n, (1024, 512, 256, 128))
    fc = _pick(f, (256, 128))
    const = lambda i: (0, 0)
    pick = lambda i: (layer, half, 0, 0)
    row = lambda i: (i, 0)
    operands, specs = [x], [pl.BlockSpec((tm, d), row)]
    if mixer is not None:
        oa, ob, w_mix, g_mix = mixer
        ca, cb = oa.shape[1], ob.shape[1]
        operands += [oa, ob, w_mix[:ca], w_mix[ca:], g_mix.reshape(1, d)]
        specs += [pl.BlockSpec((tm, ca), row), pl.BlockSpec((tm, cb), row), pl.BlockSpec((ca, d), const),
                  pl.BlockSpec((cb, d), const), pl.BlockSpec((1, d), const)]
    operands += [g_pre.reshape(1, d), g_post.reshape(1, d), wg, wu, wd]
    specs += [pl.BlockSpec((1, d), const), pl.BlockSpec((1, d), const), pl.BlockSpec((None, None, d, f), pick),
              pl.BlockSpec((None, None, d, f), pick), pl.BlockSpec((None, None, f, d), pick)]
    return pl.pallas_call(
        functools.partial(_ffn_kernel, parts=2, fc=fc, mixer=mixer is not None),
        out_shape=jax.ShapeDtypeStruct((n, d), F32),
        grid=(n // tm,),
        in_specs=specs,
        out_specs=pl.BlockSpec((tm, d), row),
        compiler_params=_params(("arbitrary",)),
        name="ffn",
    )(*operands)


def _inproj_kernel(x_ref, g_ref, w_ref, o_ref, *, parts, nc):
    rows = x_ref.shape[0] // parts
    sl = [slice(p * rows, (p + 1) * rows) for p in range(parts)]
    hs = [_rms(x_ref[s, :], g_ref[...]).astype(BF16) for s in sl]
    for c in range(w_ref.shape[1] // nc):
        cols = slice(c * nc, (c + 1) * nc)
        for p, s in enumerate(sl):
            o_ref[s, cols] = jnp.dot(hs[p], w_ref[:, cols], preferred_element_type=F32).astype(o_ref.dtype)


def _inproj(x, g, w):
    n, d = x.shape
    c = w.shape[1]
    tm = _pick(n, (1024, 512, 256, 128))
    nc = _pick(c, (512, 256, 128))
    return pl.pallas_call(
        functools.partial(_inproj_kernel, parts=2, nc=nc),
        out_shape=jax.ShapeDtypeStruct((n, c), BF16),
        grid=(n // tm,),
        in_specs=[
            pl.BlockSpec((tm, d), lambda i: (i, 0)),
            pl.BlockSpec((1, d), lambda i: (0, 0)),
            pl.BlockSpec((d, c), lambda i: (0, 0)),
        ],
        out_specs=pl.BlockSpec((tm, c), lambda i: (i, 0)),
        compiler_params=_params(("arbitrary",)),
        name="inproj",
    )(x, g.reshape(1, d), w)


def _split_pair(q):
    qf = q.astype(F32)
    lane = lax.broadcasted_iota(jnp.int32, qf.shape, 1)
    lo = jnp.where(lane < DA_HEAD_DIM, qf, 0.0).astype(BF16)
    hi = jnp.where(lane >= DA_HEAD_DIM, qf, 0.0).astype(BF16)
    return lo, hi


def _qk(q, k):
    return lax.dot_general(q, k, (((1,), (1,)), ((), ())), preferred_element_type=F32)


def _diff_kernel(q_ref, k_ref, v_ref, pos_ref, lp_ref, g_ref, o_ref, m_ref, l_ref, a_ref, kmax_ref, live_ref, *, t,
                 lambda_init):
    h = pl.program_id(1)
    qi = pl.program_id(2)
    lane = lax.broadcasted_iota(jnp.int32, (t, LANES), 1)
    head_v = (jnp.zeros((t, LANES), jnp.int32) + (h + 1)).astype(F32)
    slope = jnp.exp2(-8.0 / DA_HEADS * head_v)
    t0 = (jnp.zeros((t, LANES), jnp.int32) + qi * t).astype(F32)
    aug = jnp.where(lane == 0, slope * POS_RADIX,
                    jnp.where(lane == 1, slope, jnp.where(lane == 2, -slope * t0, 0.0))).astype(BF16)
    q_aug = [jnp.concatenate([qm, aug], axis=1) for qm in _split_pair(q_ref[0])]
    ones = jnp.ones((t, LANES), BF16)
    row_i = lax.broadcasted_iota(jnp.int32, (t, t), 0)
    col_i = lax.broadcasted_iota(jnp.int32, (t, t), 1)

    def trip(kjs, masks=None, init=False, next_kj=None):
        starts = [pl.multiple_of(kj * t, t) for kj in kjs]
        ks = [jnp.concatenate([k_ref[0, pl.ds(s0, t), :], pos_ref[pl.ds(s0, t), :]], axis=1) for s0 in starts]
        vs = [jnp.concatenate([v_ref[0, pl.ds(s0, t), :], ones], axis=1) for s0 in starts]
        sss = [[_qk(q_aug[mp], k) for k in ks] for mp in range(2)]
        if masks is not None:
            sss = [[s if mk is None else jnp.where(mk, s, -jnp.inf) for s, mk in zip(ss, masks)] for ss in sss]
        m_news = []
        for mp, ss in enumerate(sss):
            mloc = functools.reduce(jnp.maximum, [jnp.max(s, axis=-1, keepdims=True) for s in ss])
            m_new = jnp.broadcast_to(mloc, (t, LANES))
            m_news.append(m_new if init else jnp.maximum(m_ref[mp], m_new))
        pvs = []
        for ss, m_new in zip(sss, m_news):
            m_wide = jnp.concatenate([m_new] * (t // LANES), axis=1)
            pvs.append(sum(jnp.dot(jnp.exp(s - m_wide).astype(BF16), v, preferred_element_type=F32)
                           for s, v in zip(ss, vs)))
        for mp, (pv, m_new) in enumerate(zip(pvs, m_news)):
            if init:
                a_ref[mp] = pv[:, :LANES]
                l_ref[mp] = pv[:, LANES:]
            else:
                alpha = jnp.exp(m_ref[mp] - m_new)
                a_ref[mp] = alpha * a_ref[mp] + pv[:, :LANES]
                l_ref[mp] = alpha * l_ref[mp] + pv[:, LANES:]
            m_ref[mp] = m_new
        if next_kj is not None:
            live_ref[0] = alive(m_news, next_kj)
            live_ref[1] = alive(m_news, next_kj - 2)

    def alive(ms, kj):
        dist = (jnp.zeros((t, LANES), jnp.int32) + (qi - 1 - kj) * t).astype(F32)
        live = jnp.logical_or(reach - ms[0] > slope * dist, reach - ms[1] > slope * dist)
        return jnp.max(jnp.where(live, 1, 0))

    @pl.when(qi == 0)
    def _():
        ksq = [jnp.sum(jnp.square(k_ref[0, c * t:(c + 1) * t, :].astype(F32)), axis=-1, keepdims=True)
               for c in range(k_ref.shape[1] // t)]
        kmax_ref[0] = jnp.sqrt(jnp.max(functools.reduce(jnp.maximum, ksq)))
        live_ref[0] = 0
        live_ref[1] = 0

    qn = jnp.sqrt(jnp.sum(jnp.square(q_ref[0].astype(F32)), axis=-1, keepdims=True))
    reach = jnp.broadcast_to(qn * kmax_ref[0] + DIFF_DEAD, (t, LANES))
    n = jnp.maximum(qi - 1, 0)

    wide_heads = [hh for hh in range(DA_HEADS) if 2.0 ** (-8.0 / DA_HEADS * (hh + 1)) * (2 * t - 1) < DIFF_DEAD]
    wide = jnp.logical_and(qi >= 3, h >= min(wide_heads + [DA_HEADS]))

    @pl.when(qi == 0)
    def _():
        trip([qi], [col_i <= row_i], init=True)

    @pl.when(jnp.logical_and(qi > 0, jnp.logical_not(wide)))
    def _():
        trip([qi, qi - 1], [col_i <= row_i, None], init=True, next_kj=n - 1)

    @pl.when(wide)
    def _():
        trip([qi, qi - 1, qi - 2, qi - 3], [col_i <= row_i, None, None, None], init=True, next_kj=n - 3)

    first = jnp.where(wide, n - 3, n - 1)

    def quads(state):
        kj, _, _ = state
        nxt2, nxt4 = alive(m_ref, kj - 4) > 0, alive(m_ref, kj - 6) > 0
        trip([kj, kj - 1, kj - 2, kj - 3])
        return kj - 4, nxt2, nxt4

    kj, live, _ = lax.while_loop(lambda st: jnp.logical_and(st[0] >= 3, st[2]), quads,
                                 (first, live_ref[0] > 0, live_ref[1] > 0))

    def pairs(state):
        kj, _ = state
        nxt = alive(m_ref, kj - 2) > 0
        trip([kj, kj - 1])
        return kj - 2, nxt

    kj, live = lax.while_loop(lambda st: jnp.logical_and(st[0] >= 1, st[1]), pairs, (kj, live))

    @pl.when(jnp.logical_and(kj == 0, live))
    def _():
        trip([0])

    lp = lp_ref[...]
    lam = (jnp.exp(jnp.sum(lp[0:1] * lp[1:2], axis=-1, keepdims=True))
           - jnp.exp(jnp.sum(lp[2:3] * lp[3:4], axis=-1, keepdims=True)) + lambda_init)
    oa = a_ref[0] / l_ref[0] - lam * (a_ref[1] / l_ref[1])
    o_ref[0] = (_rms(oa, g_ref[...]) * (1.0 - lambda_init)).astype(o_ref.dtype)


def _key_positions(s):
    pos = jnp.arange(s, dtype=jnp.int32)[:, None]
    lane = jnp.arange(LANES, dtype=jnp.int32)[None, :]
    tab = jnp.where(lane == 0, pos // POS_RADIX, jnp.where(lane == 1, pos % POS_RADIX, (lane == 2).astype(jnp.int32)))
    return tab.astype(BF16)


def _diff_attn(proj, lam_p, g_subln, lambda_init):
    b, s, _ = proj.shape
    t = _pick(s, (ATTN_TILE, LANES))
    assert s // POS_RADIX <= 256 and s // t <= 256
    kern = functools.partial(_diff_kernel, t=t, lambda_init=lambda_init)
    return pl.pallas_call(
        kern,
        out_shape=jax.ShapeDtypeStruct((b, s, DA_HEADS * LANES), BF16),
        grid=(b, DA_HEADS, s // t),
        in_specs=[
            pl.BlockSpec((1, t, LANES), lambda bi, h, qi: (bi, qi, h)),
            pl.BlockSpec((1, s, LANES), lambda bi, h, qi: (bi, 0, DA_HEADS + h)),
            pl.BlockSpec((1, s, LANES), lambda bi, h, qi: (bi, 0, 2 * DA_HEADS + h)),
            pl.BlockSpec((s, LANES), lambda bi, h, qi: (0, 0)),
            pl.BlockSpec(lam_p.shape, lambda bi, h, qi: (0, 0)),
            pl.BlockSpec((1, LANES), lambda bi, h, qi: (0, 0)),
        ],
        out_specs=pl.BlockSpec((1, t, LANES), lambda bi, h, qi: (bi, qi, h)),
        scratch_shapes=[pltpu.VMEM((2, t, LANES), F32)] * 3 + [pltpu.SMEM((1,), F32), pltpu.SMEM((2,), jnp.int32)],
        compiler_params=_params(("arbitrary", "arbitrary", "arbitrary")),
        name="diff_attn",
    )(proj, proj, proj, _key_positions(s), lam_p, g_subln.reshape(1, LANES))


def _sb_kernel(q_ref, k_ref, v_ref, w_ref, o_ref, c_ref, a_ref, live_ref, *, tq, tk):
    qi = pl.program_id(2)
    ratio = tq // tk
    nsub = tk // SB_BLOCK
    qs = _split_pair(q_ref[0])
    row_i = lax.broadcasted_iota(jnp.int32, (tq, SB_BLOCK), 0)
    col_i = lax.broadcasted_iota(jnp.int32, (tq, SB_BLOCK), 1)

    def heavy(kjs, diags=None):
        diags = diags or [None] * len(kjs)
        r0s = [dg or 0 for dg in diags]
        starts = [pl.multiple_of(kj * tk, tk) for kj in kjs]
        ks = [k_ref[0, pl.ds(s0, tk), :] for s0 in starts]
        vs = [v_ref[0, pl.ds(s0, tk), :] for s0 in starts]
        wmat = w_ref[...]
        chains = [(ti, hh) for ti in range(len(kjs)) for hh in range(2)]
        zs = [_qk(qs[hh][r0s[ti]:], ks[ti]) for ti, hh in chains]
        later = [None] * len(chains)
        wts = [[None] * nsub for _ in chains]
        for sb in reversed(range(nsub)):
            zbs = [z[:, sb * SB_BLOCK:(sb + 1) * SB_BLOCK] for z in zs]
            stricts = [None if dg is None else (col_i + (dg + sb * SB_BLOCK) < row_i)[dg:] for dg in diags]
            sps = []
            for (ti, hh), zb in zip(chains, zbs):
                sp = jnp.maximum(zb, 0.0) + jnp.log(1.0 + jnp.exp2(-jnp.abs(zb))) * LOG2E
                sps.append((sp if stricts[ti] is None else jnp.where(stricts[ti], sp, 0.0)).astype(BF16))
            ces = [jnp.dot(sp, wmat, preferred_element_type=F32) for sp in sps]
            for c, (zb, ce) in enumerate(zip(zbs, ces)):
                strict = stricts[chains[c][0]]
                arg = zb + ce
                if later[c] is not None:
                    arg = arg + jnp.concatenate([later[c]] * (SB_BLOCK // LANES), axis=1)
                w = jnp.exp2(arg)
                wts[c][sb] = (w if strict is None else jnp.where(strict, w, 0.0)).astype(BF16)
                total = jnp.broadcast_to(ce[:, 0:1], (ce.shape[0], LANES))
                later[c] = total if later[c] is None else later[c] + total
        pvs = [jnp.dot(wts[c][0] if nsub == 1 else jnp.concatenate(wts[c], axis=1), vs[ti],
                       preferred_element_type=F32) for c, (ti, hh) in enumerate(chains)]
        return [(pv, total, r0s[ti]) for pv, total, (ti, hh) in zip(pvs, later, chains)]

    def combine(outs, init=False):
        carry = [None, None] if init else [c_ref[0], c_ref[1]]
        before = []
        for c, (pv, total, r0) in enumerate(outs):
            hh = c % 2
            before.append(carry[hh])
            if r0:
                total = jnp.concatenate([jnp.zeros((r0, LANES), F32), total], axis=0)
            carry[hh] = total if carry[hh] is None else carry[hh] + total
        c_ref[0], c_ref[1] = carry
        live_ref[0] = jnp.max(jnp.where(jnp.maximum(carry[0], carry[1]) > SB_DEAD, 1, 0))
        for c, (pv, total, r0) in enumerate(outs):
            hh = c % 2
            if before[c] is None:
                if r0:
                    a_ref[hh, :r0] = jnp.zeros((r0, LANES), F32)
                a_ref[hh, r0:] = pv
            else:
                a_ref[hh, r0:] += jnp.exp2(before[c][r0:]) * pv

    n = qi * ratio
    diag_kjs = [n + i for i in reversed(range(ratio))]
    diag_offs = [i * tk for i in reversed(range(ratio))]

    @pl.when(qi == 0)
    def _():
        combine(heavy(diag_kjs, diag_offs), init=True)

    @pl.when(qi > 0)
    def _():
        combine(heavy(diag_kjs + [n - 1], diag_offs + [None]), init=True)

    m = jnp.maximum(n - 1, 0)

    def body(state):
        i, _ = state
        combine(heavy([m - 1 - 2 * i, m - 2 - 2 * i]))
        return i + 1, live_ref[0] > 0

    _, live = lax.while_loop(lambda st: jnp.logical_and(st[0] < m // 2, st[1]), body, (0, live_ref[0] > 0))

    @pl.when(jnp.logical_and(m % 2 == 1, live))
    def _():
        combine(heavy([0]))

    lane = lax.broadcasted_iota(jnp.int32, (tq, LANES), 1)
    o_ref[0] = jnp.where(lane < SB_HEAD_DIM, a_ref[0], a_ref[1]).astype(o_ref.dtype)


def _suffix_matrix():
    j = jnp.arange(SB_BLOCK)[:, None]
    s = jnp.arange(SB_BLOCK)[None, :]
    return -(j >= s).astype(BF16)


def _sb_attn(proj):
    b, s, _ = proj.shape
    t = _pick(s, (ATTN_TILE, SB_BLOCK))
    tk = SB_BLOCK
    base = 3 * DA_HEADS
    kern = functools.partial(_sb_kernel, tq=t, tk=tk)
    return pl.pallas_call(
        kern,
        out_shape=jax.ShapeDtypeStruct((b, s, SB_PAIRS * LANES), BF16),
        grid=(b, SB_PAIRS, s // t),
        in_specs=[
            pl.BlockSpec((1, t, LANES), lambda bi, p, qi: (bi, qi, base + p)),
            pl.BlockSpec((1, s, LANES), lambda bi, p, qi: (bi, 0, base + SB_PAIRS + p)),
            pl.BlockSpec((1, s, LANES), lambda bi, p, qi: (bi, 0, base + 2 * SB_PAIRS + p)),
            pl.BlockSpec((SB_BLOCK, SB_BLOCK), lambda bi, p, qi: (0, 0)),
        ],
        out_specs=pl.BlockSpec((1, t, LANES), lambda bi, p, qi: (bi, qi, p)),
        scratch_shapes=[pltpu.VMEM((2, t, LANES), F32)] * 2 + [pltpu.SMEM((1,), jnp.int32)],
        compiler_params=_params(("arbitrary", "arbitrary", "arbitrary")),
        name="sb_attn",
    )(proj, proj, proj, _suffix_matrix())


def _sgu_kernel(x_ref, gpre_ref, gpost_ref, wuv_ref, buv_ref, gln_ref, bln_ref, wsp_ref, bsp_ref, wo_ref, o_ref,
                *, parts, width):
    rows = x_ref.shape[0] // parts
    sl = [slice(p * rows, (p + 1) * rows) for p in range(parts)]
    t_i = lax.broadcasted_iota(jnp.int32, (SGU_CHUNK, SGU_CHUNK), 0)
    s_i = lax.broadcasted_iota(jnp.int32, (SGU_CHUNK, SGU_CHUNK), 1)
    gch = width // SGU_GROUPS
    w_tri = [jnp.where(s_i <= t_i, wsp_ref[g], 0.0).astype(BF16) for g in range(SGU_GROUPS)]

    hs = [_rms(x_ref[s, :], gpre_ref[...]).astype(BF16) for s in sl]
    zzs = [jnp.dot(h, wuv_ref[...], preferred_element_type=F32) for h in hs]
    us, vs = [], []
    for zz in zzs:
        zz = _gelu_tanh(zz + buv_ref[...])
        v = zz[:, width:]
        vc = v - jnp.mean(v, axis=-1, keepdims=True)
        vs.append((vc * lax.rsqrt(jnp.mean(vc * vc, axis=-1, keepdims=True) + EPS) * gln_ref[...]
                   + bln_ref[...]).astype(BF16))
        us.append(zz[:, :width])
    mixeds = [jnp.concatenate(
        [jnp.concatenate([jnp.dot(w_tri[g], v[n * SGU_CHUNK:(n + 1) * SGU_CHUNK, g * gch:(g + 1) * gch],
                                  preferred_element_type=F32) for n in range(rows // SGU_CHUNK)], axis=0)
         for g in range(SGU_GROUPS)], axis=1) for v in vs]
    ys = [jnp.dot((u * (mixed + bsp_ref[...])).astype(BF16), wo_ref[...], preferred_element_type=F32)
          for u, mixed in zip(us, mixeds)]
    for s, y in zip(sl, ys):
        o_ref[s, :] = x_ref[s, :] + _rms(y, gpost_ref[...])


def _sgu(x, g_pre, g_post, w_uv, b_uv, g_ln, b_ln, w_sp, b_sp, w_out):
    n, d = x.shape
    width = w_out.shape[0]
    tm = _pick(n, (1024, 512, 256))
    parts = 2
    rows = tm // parts
    gch = width // SGU_GROUPS
    bias = jnp.tile(jnp.repeat(b_sp.T, gch, axis=1), (rows // SGU_CHUNK, 1))
    kern = functools.partial(_sgu_kernel, parts=parts, width=width)
    const = lambda i: (0, 0)
    return pl.pallas_call(
        kern,
        out_shape=jax.ShapeDtypeStruct((n, d), F32),
        grid=(n // tm,),
        in_specs=[
            pl.BlockSpec((tm, d), lambda i: (i, 0)),
            pl.BlockSpec((1, d), const),
            pl.BlockSpec((1, d), const),
            pl.BlockSpec((d, 2 * width), const),
            pl.BlockSpec((1, 2 * width), const),
            pl.BlockSpec((1, width), const),
            pl.BlockSpec((1, width), const),
            pl.BlockSpec(w_sp.shape, lambda i: (0, 0, 0)),
            pl.BlockSpec((rows, width), const),
            pl.BlockSpec((width, d), const),
        ],
        out_specs=pl.BlockSpec((tm, d), lambda i: (i, 0)),
        compiler_params=_params(("arbitrary",)),
        name="sgu",
    )(x, g_pre.reshape(1, d), g_post.reshape(1, d), w_uv, b_uv.reshape(1, -1), g_ln.reshape(1, -1),
      b_ln.reshape(1, -1), w_sp, bias, w_out)


def _q_scales():
    da_w, sb_w = DA_HEADS * LANES, SB_PAIRS * LANES
    one = lambda n: jnp.ones((n,), F32)
    return jnp.concatenate([one(da_w) * DA_HEAD_DIM ** -0.5, one(2 * da_w),
                            one(sb_w) * (LOG2E * SB_HEAD_DIM ** -0.5), one(2 * sb_w)])


def kernel(x, g_norm, w_ffn_gate, w_ffn_up, w_ffn_down, w_in_ab, w_out_ab, lambda_params, g_subln,
           w_uv, b_uv, g_sgu_ln, b_sgu_ln, w_spatial, b_spatial, w_out_c):
    b, s, d = x.shape
    depth = g_norm.shape[0]
    assert DA_HEAD_DIM == SB_HEAD_DIM == LANES // 2 and w_in_ab.shape[2] == 3 * LANES * (DA_HEADS + SB_PAIRS)
    assert g_subln.shape[1] == LANES and w_spatial.shape[1:] == (SGU_GROUPS, SGU_CHUNK, SGU_CHUNK)
    cast = lambda w: w.astype(BF16)
    wg_all, wu_all, wd_all = cast(w_ffn_gate), cast(w_ffn_up), cast(w_ffn_down)
    xf = x.reshape(b * s, d)
    for l in range(depth):
        g = g_norm[l]
        xf = _ffn(xf, g[0], g[1], wg_all, wu_all, wd_all, l, 0)
        if l % 2 == 0:
            e = l // 2
            lambda_init = 0.8 - 0.6 * math.exp(-0.3 * l)
            proj = _inproj(xf, g[2], cast(w_in_ab[e] * _q_scales())).reshape(b, s, -1)
            oa = _diff_attn(proj, lambda_params[e], g_subln[e], lambda_init)
            ob = _sb_attn(proj)
            mixer = (oa.reshape(b * s, -1), ob.reshape(b * s, -1), cast(w_out_ab[e]), g[3])
        else:
            o = l // 2
            xf = _sgu(xf, g[2], g[3], cast(w_uv[o]), b_uv[o], g_sgu_ln[o], b_sgu_ln[o], w_spatial[o],
                      b_spatial[o], cast(w_out_c[o]))
            mixer = None
        xf = _ffn(xf, g[4], g[5], wg_all, wu_all, wd_all, l, 1, mixer)
    return xf.reshape(b, s, d)
```
